```python
import jax, jax.numpy as jnp
from jax import lax
import numpy as np

D_MODEL = 4096
BATCH = 4
SEQ = 4096
DEPTH = 1

D_MIX = D_MODEL
RWKV_W = D_MIX // 2
CONV_W = D_MIX - RWKV_W
HEAD_SIZE = 64
N_HEADS = RWKV_W // HEAD_SIZE
LORA_W = 96
LORA_A = 96
CONV_K = 31
NORM_EPS = 1e-6
LN_EPS = 1e-5
GN_EPS = 1e-5 * HEAD_SIZE
SHIFT_COLS = 3 * RWKV_W + LORA_W + LORA_A
IN_COLS = SHIFT_COLS + RWKV_W + 2 * CONV_W + CONV_W

kernel_name = "hybrid_rwkv7_conformer_parallel"


def rms_norm(x, g):
    xf = x.astype(jnp.float32)
    y = xf * lax.rsqrt(jnp.mean(xf * xf, axis=-1, keepdims=True) + NORM_EPS)
    return (y * g.astype(jnp.float32)).astype(x.dtype)


def layer_norm(x, g, b):
    xf = x.astype(jnp.float32)
    mu = jnp.mean(xf, axis=-1, keepdims=True)
    var = jnp.mean(jnp.square(xf - mu), axis=-1, keepdims=True)
    y = (xf - mu) * lax.rsqrt(var + LN_EPS)
    return (y * g.astype(jnp.float32) + b.astype(jnp.float32)).astype(x.dtype)


def rwkv7_recurrence(r, w, k, v, kk, a):
    B, T, H, N = r.shape
    tm = lambda z: jnp.moveaxis(z, 1, 0)
    seq = (tm(r), tm(w), tm(k), tm(v), tm(kk), tm(kk * a))

    def step(S, inp):
        r_t, w_t, k_t, v_t, kk_t, b_t = inp
        sa = jnp.einsum('bhij,bhj->bhi', S, -kk_t)
        S = S * w_t[:, :, None, :] + sa[..., None] * b_t[:, :, None, :] + v_t[..., None] * k_t[:, :, None, :]
        y_t = jnp.einsum('bhij,bhj->bhi', S, r_t)
        return S, y_t

    S0 = jnp.zeros((B, H, N, N), jnp.float32)
    _, ys = lax.scan(step, S0, seq)
    return jnp.moveaxis(ys, 0, 1)


def setup_inputs(seed: int = 0) -> dict:
    key = jax.random.key(seed)
    ks = jax.random.split(key, 24)
    f32 = jnp.float32
    nrm = lambda k, s, sc: jax.random.normal(k, s, f32) * sc
    return {
        "x": nrm(ks[0], (BATCH, SEQ, D_MODEL), 1.0),
        "norm_pre_g": 1.0 + nrm(ks[1], (D_MODEL,), 0.02),
        "w_in": nrm(ks[2], (D_MODEL, IN_COLS), D_MODEL ** -0.5),
        "mu_shift": jax.random.uniform(ks[3], (SHIFT_COLS,), f32, 0.0, 1.0),
        "w0": jax.random.uniform(ks[4], (RWKV_W,), f32, -5.5, -0.5),
        "w_lora_up": nrm(ks[5], (LORA_W, RWKV_W), 0.5 * LORA_W ** -0.5),
        "a0": nrm(ks[6], (RWKV_W,), 0.1),
        "a_lora_up": nrm(ks[7], (LORA_A, RWKV_W), 0.5 * LORA_A ** -0.5),
        "k_k": 0.85 + nrm(ks[8], (RWKV_W,), 0.02),
        "k_a": 1.0 + nrm(ks[9], (RWKV_W,), 0.02),
        "r_k": nrm(ks[10], (N_HEADS, HEAD_SIZE), 0.1),
        "lnx_g": 1.0 + nrm(ks[11], (RWKV_W,), 0.02),
        "lnx_b": nrm(ks[12], (RWKV_W,), 0.01),
        "conv_w": nrm(ks[13], (CONV_K, CONV_W), CONV_K ** -0.5),
        "conv_b": nrm(ks[14], (CONV_W,), 0.01),
        "cln_g": 1.0 + nrm(ks[15], (CONV_W,), 0.02),
        "cln_b": nrm(ks[16], (CONV_W,), 0.01),
        "w_pw2": nrm(ks[17], (CONV_W, CONV_W), CONV_W ** -0.5),
        "b_pw2": nrm(ks[18], (CONV_W,), 0.01),
        "w_out": nrm(ks[19], (D_MIX, D_MODEL), D_MIX ** -0.5),
        "norm_post_g": 1.0 + nrm(ks[20], (D_MODEL,), 0.02),
    }


def hybrid_layer(x, norm_pre_g, w_in, mu_shift, w0, w_lora_up, a0, a_lora_up, k_k, k_a, r_k,
                 lnx_g, lnx_b, conv_w, conv_b, cln_g, cln_b, w_pw2, b_pw2, w_out, norm_post_g):
    B, T, _ = x.shape
    f32 = jnp.float32
    h = rms_norm(x, norm_pre_g)
    proj = jnp.einsum('btd,dc->btc', h, w_in)
    c0 = SHIFT_COLS
    c1 = c0 + RWKV_W
    c2 = c1 + CONV_W
    c3 = c2 + CONV_W
    rwkv_in, g_rwkv, glu_v, glu_g, g_conv = jnp.split(proj, [c0, c1, c2, c3], axis=-1)

    prev = jnp.pad(rwkv_in, ((0, 0), (1, 0), (0, 0)))[:, :-1]
    xs = rwkv_in + (prev - rwkv_in) * mu_shift
    r, k, v, w_low, a_low = jnp.split(xs, [RWKV_W, 2 * RWKV_W, 3 * RWKV_W, 3 * RWKV_W + LORA_W], axis=-1)
    w_log = -jax.nn.softplus(-(w0 + jnp.tanh(w_low) @ w_lora_up).astype(f32)) - 0.5
    decay = jnp.exp(-jnp.exp(w_log))
    a = jax.nn.sigmoid((a0 + a_low @ a_lora_up).astype(f32))
    hs = lambda z: z.astype(f32).reshape(B, T, N_HEADS, HEAD_SIZE)
    r_h, k_f, v_h, a_h, w_h = hs(r), hs(k), hs(v), hs(a), hs(decay)
    k_k_h = k_k.astype(f32).reshape(N_HEADS, HEAD_SIZE)
    k_a_h = k_a.astype(f32).reshape(N_HEADS, HEAD_SIZE)
    kk = k_f * k_k_h
    kk = kk / jnp.maximum(jnp.linalg.norm(kk, axis=-1, keepdims=True), 1e-12)
    k_h = k_f * (1.0 + (a_h - 1.0) * k_a_h)
    y = rwkv7_recurrence(r_h, w_h, k_h, v_h, kk, a_h)
    mu = jnp.mean(y, axis=-1, keepdims=True)
    var = jnp.mean(jnp.square(y - mu), axis=-1, keepdims=True)
    y = ((y - mu) * lax.rsqrt(var + GN_EPS)).reshape(B, T, RWKV_W)
    y = y * lnx_g.astype(f32) + lnx_b.astype(f32)
    bonus = jnp.sum(r_h * k_h * r_k.astype(f32), axis=-1, keepdims=True) * v_h
    y = (y + bonus.reshape(B, T, RWKV_W)).astype(x.dtype)
    y_rwkv = y * jax.nn.silu(g_rwkv)

    u = glu_v * jax.nn.sigmoid(glu_g)
    u_pad = jnp.pad(u, ((0, 0), (CONV_K - 1, 0), (0, 0)))
    c = lax.conv_general_dilated(u_pad, conv_w[:, None, :].astype(u.dtype), window_strides=(1,),
                                 padding='VALID', dimension_numbers=('NWC', 'WIO', 'NWC'),
                                 feature_group_count=CONV_W) + conv_b
    c = jax.nn.silu(layer_norm(c, cln_g, cln_b))
    c = jnp.einsum('btc,ce->bte', c, w_pw2) + b_pw2
    y_conv = c * jax.nn.silu(g_conv)

    mix = jnp.concatenate([y_rwkv, y_conv], axis=-1)
    out = jnp.einsum('btc,cd->btd', mix, w_out)
    return x + rms_norm(out, norm_post_g)


def reference(x, norm_pre_g, w_in, mu_shift, w0, w_lora_up, a0, a_lora_up, k_k, k_a, r_k,
              lnx_g, lnx_b, conv_w, conv_b, cln_g, cln_b, w_pw2, b_pw2, w_out, norm_post_g):
    for _ in range(DEPTH):
        x = hybrid_layer(x, norm_pre_g, w_in, mu_shift, w0, w_lora_up, a0, a_lora_up, k_k, k_a, r_k,
                         lnx_g, lnx_b, conv_w, conv_b, cln_g, cln_b, w_pw2, b_pw2, w_out, norm_post_g)
    return x
```

```python
import functools

import jax
import jax.numpy as jnp
from jax import lax
from jax.experimental import pallas as pl
from jax.experimental.pallas import tpu as pltpu

F32 = jnp.float32
BF16 = jnp.bfloat16

HEAD_SIZE = 64
LORA_W = 96
LORA_A = 96
CONV_K = 31
NORM_EPS = 1e-6
LN_EPS = 1e-5
GN_EPS = 1e-5 * HEAD_SIZE

LANES = 128
LORA_PAD = 256
CHUNK = 64
VMEM_LIMIT = 56 * 1024 * 1024

HI = lax.Precision.HIGHEST


def _cparams(sem):
    return pltpu.CompilerParams(dimension_semantics=sem, vmem_limit_bytes=VMEM_LIMIT)


def _prenorm_kernel(x_ref, g_ref, o_ref):
    x = x_ref[...]
    ms = jnp.mean(x * x, axis=-1, keepdims=True)
    o_ref[...] = (x * lax.rsqrt(ms + NORM_EPS) * g_ref[...]).astype(o_ref.dtype)


def _prenorm(x2, g, tm=256):
    m, d = x2.shape
    return pl.pallas_call(
        _prenorm_kernel,
        out_shape=jax.ShapeDtypeStruct((m, d), BF16),
        grid=(m // tm,),
        in_specs=[pl.BlockSpec((tm, d), lambda i: (i, 0)),
                  pl.BlockSpec((1, d), lambda i: (0, 0))],
        out_specs=pl.BlockSpec((tm, d), lambda i: (i, 0)),
        compiler_params=_cparams(("parallel",)),
        name="prenorm",
    )(x2, g.reshape(1, d))


def _mm_shift_kernel(h_ref, w_ref, mu_ref, o_ref, carry_ref, *, tiles_per_seq):
    i = pl.program_id(1)
    p = jnp.dot(h_ref[...], w_ref[...], preferred_element_type=F32)

    @pl.when(i % tiles_per_seq == 0)
    def _():
        carry_ref[...] = jnp.zeros_like(carry_ref)

    rows = lax.broadcasted_iota(jnp.int32, p.shape, 0)
    prev = jnp.where(rows == 0, carry_ref[...], pltpu.roll(p, 1, axis=0))
    carry_ref[...] = p[-1:, :]
    o_ref[...] = p + (prev - p) * mu_ref[...]


def _mm_shift(h, w, mu, seq_len, tm=1024, tn=640):
    m, k = h.shape
    n = w.shape[1]
    tm = min(tm, seq_len)
    kern = functools.partial(_mm_shift_kernel, tiles_per_seq=seq_len // tm)
    return pl.pallas_call(
        kern,
        out_shape=jax.ShapeDtypeStruct((m, n), F32),
        grid=(n // tn, m // tm),
        in_specs=[pl.BlockSpec((tm, k), lambda j, i: (i, 0)),
                  pl.BlockSpec((k, tn), lambda j, i: (0, j)),
                  pl.BlockSpec((1, tn), lambda j, i: (0, j))],
        out_specs=pl.BlockSpec((tm, tn), lambda j, i: (i, j)),
        scratch_shapes=[pltpu.VMEM((1, tn), F32)],
        compiler_params=_cparams(("parallel", "arbitrary")),
        name="inproj_shift",
    )(h, w, mu)


def _mm_silu_kernel(h_ref, w_ref, o_ref):
    p = jnp.dot(h_ref[...], w_ref[...], preferred_element_type=F32)
    o_ref[...] = p * jax.nn.sigmoid(p)


def _mm_silu(h, w, tm=1024, tn=512):
    m, k = h.shape
    n = w.shape[1]
    tm = min(tm, m)
    return pl.pallas_call(
        _mm_silu_kernel,
        out_shape=jax.ShapeDtypeStruct((m, n), F32),
        grid=(m // tm, n // tn),
        in_specs=[pl.BlockSpec((tm, k), lambda i, j: (i, 0)),
                  pl.BlockSpec((k, tn), lambda i, j: (0, j))],
        out_specs=pl.BlockSpec((tm, tn), lambda i, j: (i, j)),
        compiler_params=_cparams(("parallel", "arbitrary")),
        name="inproj_silu",
    )(h, w)


def _mm_glu_kernel(h_ref, wv_ref, wg_ref, o_ref):
    h = h_ref[...]
    pv = jnp.dot(h, wv_ref[...], preferred_element_type=F32)
    pg = jnp.dot(h, wg_ref[...], preferred_element_type=F32)
    o_ref[...] = pv * jax.nn.sigmoid(pg)


def _mm_glu(h, wv, wg, tm=1024, tn=256):
    m, k = h.shape
    n = wv.shape[1]
    tm = min(tm, m)
    return pl.pallas_call(
        _mm_glu_kernel,
        out_shape=jax.ShapeDtypeStruct((m, n), F32),
        grid=(m // tm, n // tn),
        in_specs=[pl.BlockSpec((tm, k), lambda i, j: (i, 0)),
                  pl.BlockSpec((k, tn), lambda i, j: (0, j)),
                  pl.BlockSpec((k, tn), lambda i, j: (0, j))],
        out_specs=pl.BlockSpec((tm, tn), lambda i, j: (i, j)),
        compiler_params=_cparams(("parallel", "arbitrary")),
        name="inproj_glu",
    )(h, wv, wg)


def _dot(a, b):
    return jnp.dot(a, b, preferred_element_type=F32, precision=HI)


def _dot_nt(a, b):
    return lax.dot_general(a, b, (((1,), (1,)), ((), ())),
                           preferred_element_type=F32, precision=HI)


def _unit_lower_inverse(a, rows, cols):
    eye = (rows == cols).astype(F32)
    same8 = (rows // 8) == (cols // 8)
    a8 = jnp.where(same8, a, 0.0)
    a8_2 = _dot(a8, a8)
    a8_4 = _dot(a8_2, a8_2)
    p1 = eye + a8
    p1 = p1 + _dot(p1, a8_2)
    t = p1 + _dot(p1, a8_4)
    blk = 8
    while blk < CHUNK:
        inner = (rows // blk) == (cols // blk)
        outer = (rows // (2 * blk)) == (cols // (2 * blk))
        e = jnp.where(outer & jnp.logical_not(inner), a, 0.0)
        t = t + _dot(_dot(t, e), t)
        blk *= 2
    return t


def _rwkv_kernel(r_ref, k_ref, v_ref, lora_ref, gate_ref, w0_ref, a0_ref, kk_ref, ka_ref,
                 rk_ref, lng_ref, lnb_ref, wup_ref, aup_ref, o_ref, s_ref, y_ref, *, n_chunks):
    @pl.when(pl.program_id(2) == 0)
    def _():
        s_ref[...] = jnp.zeros_like(s_ref)

    two_c = 2 * CHUNK
    lane = lax.broadcasted_iota(jnp.int32, (1, LANES), 1)
    m0 = (lane < HEAD_SIZE).astype(F32)
    m1 = 1.0 - m0
    rows = lax.broadcasted_iota(jnp.int32, (two_c, two_c), 0)
    cols = lax.broadcasted_iota(jnp.int32, (two_c, two_c), 1)
    same_head = (rows // CHUNK) == (cols // CHUNK)
    strict = same_head & (cols < rows)
    incl = same_head & (cols <= rows)
    eye = (rows == cols).astype(F32)
    seg_ones = same_head.astype(F32)
    cr = lax.broadcasted_iota(jnp.int32, (CHUNK, CHUNK), 0)
    cc = lax.broadcasted_iota(jnp.int32, (CHUNK, CHUNK), 1)
    cum_l = (cc <= cr).astype(F32)

    r = r_ref[...]
    k = k_ref[...]
    v = v_ref[...]
    lo = lora_ref[...]
    zw = w0_ref[...] + _dot(jnp.tanh(lo), wup_ref[...])
    nz = -zw
    softplus = jnp.maximum(nz, 0.0) + jnp.log(1.0 + jnp.exp(-jnp.abs(nz)))
    w_log = -softplus - 0.5
    lw = -jnp.exp(w_log)
    alr = jax.nn.sigmoid(a0_ref[...] + _dot(lo, aup_ref[...]))
    kkr = k * kk_ref[...]
    nrm = jnp.sqrt(_dot(kkr * kkr, seg_ones))
    kk = kkr / jnp.maximum(nrm, 1e-12)
    kh = k * (1.0 + (alr - 1.0) * ka_ref[...])
    bb = kk * alr
    aa = -kk
    bonus = _dot(r * kh * rk_ref[...], seg_ones) * v

    def stack(z):
        return jnp.concatenate([z * m0, z * m1], axis=0)

    for c in range(n_chunks):
        sl = slice(c * CHUNK, (c + 1) * CHUNK)
        lw_c = lw[sl]
        g = _dot(cum_l, lw_c)
        ge = g - lw_c
        g_end = g[CHUNK - 1:CHUNK, :]
        e_pos = jnp.exp(g)
        e_neg = jnp.exp(-g)
        e_tail = jnp.exp(g_end - g)
        r_t = stack(r[sl] * e_pos)
        a_t = stack(aa[sl] * jnp.exp(ge))
        b_t = stack(bb[sl] * e_neg)
        k_t = stack(kh[sl] * e_neg)
        b_h = stack(bb[sl] * e_tail)
        k_h = stack(kh[sl] * e_tail)
        v_s = stack(v[sl])

        gram = _dot_nt(jnp.concatenate([a_t, r_t], axis=0), jnp.concatenate([b_t, k_t], axis=0))
        a_ab = jnp.where(strict, gram[:two_c, :two_c], 0.0)
        a_ak = jnp.where(strict, gram[:two_c, two_c:], 0.0)
        a_rb = jnp.where(incl, gram[two_c:, :two_c], 0.0)
        a_rk = jnp.where(incl, gram[two_c:, two_c:], 0.0)

        t_inv = _unit_lower_inverse(a_ab, rows, cols)
        w1 = _dot(t_inv, a_t)
        w2 = _dot(t_inv, _dot(a_ak, v_s))
        m_c = eye * jnp.exp(g_end) + _dot(w1.T, b_h)
        n_c = _dot(w2.T, b_h) + _dot(v_s.T, k_h)

        s = s_ref[...]
        us = _dot_nt(jnp.concatenate([w1, r_t], axis=0), s)
        u_s = us[:two_c] + w2
        y_s = us[two_c:] + _dot(a_rb, u_s) + _dot(a_rk, v_s)
        s_ref[...] = _dot(s, m_c) + n_c
        y_ref[sl, :] = y_s[:CHUNK] + y_s[CHUNK:]

    y = y_ref[...]
    inv_n = 1.0 / HEAD_SIZE
    mu = _dot(y, seg_ones) * inv_n
    d = y - mu
    var = _dot(d * d, seg_ones) * inv_n
    yn = d * lax.rsqrt(var + GN_EPS) * lng_ref[...] + lnb_ref[...]
    o_ref[...] = ((yn + bonus) * gate_ref[...]).astype(o_ref.dtype)


def _rwkv(p3, gates3, w0, a0, k_k, k_a, r_k, lnx_g, lnx_b, wup, aup, rwkv_w, tt=256):
    b, t, _ = p3.shape
    n_pairs = rwkv_w // LANES
    lora_blk = (3 * rwkv_w) // LORA_PAD
    vec = lambda z: z.reshape(1, rwkv_w).astype(F32)
    col = lambda off: pl.BlockSpec((None, tt, LANES), lambda bi, hp, ti: (bi, ti, off + hp))
    par = pl.BlockSpec((1, LANES), lambda bi, hp, ti: (0, hp))
    up = pl.BlockSpec((LORA_PAD, LANES), lambda bi, hp, ti: (0, hp))
    kern = functools.partial(_rwkv_kernel, n_chunks=tt // CHUNK)
    return pl.pallas_call(
        kern,
        out_shape=jax.ShapeDtypeStruct((b, t, rwkv_w), BF16),
        grid=(b, n_pairs, t // tt),
        in_specs=[col(0), col(n_pairs), col(2 * n_pairs),
                  pl.BlockSpec((None, tt, LORA_PAD), lambda bi, hp, ti: (bi, ti, lora_blk)),
                  pl.BlockSpec((None, tt, LANES), lambda bi, hp, ti: (bi, ti, hp)),
                  par, par, par, par, par, par, par, up, up],
        out_specs=pl.BlockSpec((None, tt, LANES), lambda bi, hp, ti: (bi, ti, hp)),
        scratch_shapes=[pltpu.VMEM((LANES, LANES), F32), pltpu.VMEM((tt, LANES), F32)],
        compiler_params=_cparams(("parallel", "parallel", "arbitrary")),
        name="rwkv7",
    )(p3, p3, p3, p3, gates3, vec(w0), vec(a0), vec(k_k), vec(k_a), vec(r_k), vec(lnx_g),
      vec(lnx_b), wup, aup)


HALO = 32


def _conv_kernel(u_ref, gate_ref, cw_ref, cb_ref, lg_ref, lb_ref, w_ref, b_ref, o_ref, buf_ref,
                 *, tt):
    @pl.when(pl.program_id(1) == 0)
    def _():
        buf_ref[0:HALO, :] = jnp.zeros((HALO, buf_ref.shape[1]), F32)

    buf_ref[HALO:HALO + tt, :] = u_ref[...]
    base = HALO - (CONV_K - 1)
    acc = jnp.zeros((tt, buf_ref.shape[1]), F32) + cb_ref[...]
    for j in range(CONV_K):
        acc = acc + buf_ref[base + j:base + j + tt, :] * cw_ref[j:j + 1, :]
    tail = buf_ref[tt:tt + HALO, :]
    buf_ref[0:HALO, :] = tail

    mu = jnp.mean(acc, axis=-1, keepdims=True)
    d = acc - mu
    var = jnp.mean(d * d, axis=-1, keepdims=True)
    c = d * lax.rsqrt(var + LN_EPS) * lg_ref[...] + lb_ref[...]
    c = c * jax.nn.sigmoid(c)
    z = jnp.dot(c.astype(BF16), w_ref[...], preferred_element_type=F32) + b_ref[...]
    o_ref[...] = (z * gate_ref[...]).astype(o_ref.dtype)


def _conv(u3, gates3, conv_w, conv_b, cln_g, cln_b, w_pw2, b_pw2, gate_col_blk, tt=256):
    b, t, cw = u3.shape
    vec = lambda z: z.reshape(1, cw).astype(F32)
    par = pl.BlockSpec((1, cw), lambda bi, ti: (0, 0))
    kern = functools.partial(_conv_kernel, tt=tt)
    return pl.pallas_call(
        kern,
        out_shape=jax.ShapeDtypeStruct((b, t, cw), BF16),
        grid=(b, t // tt),
        in_specs=[pl.BlockSpec((None, tt, cw), lambda bi, ti: (bi, ti, 0)),
                  pl.BlockSpec((None, tt, cw), lambda bi, ti: (bi, ti, gate_col_blk)),
                  pl.BlockSpec((CONV_K, cw), lambda bi, ti: (0, 0)),
                  par, par, par,
                  pl.BlockSpec((cw, cw), lambda bi, ti: (0, 0)),
                  par],
        out_specs=pl.BlockSpec((None, tt, cw), lambda bi, ti: (bi, ti, 0)),
        scratch_shapes=[pltpu.VMEM((HALO + tt, cw), F32)],
        compiler_params=_cparams(("parallel", "arbitrary")),
        name="conv_group",
    )(u3, gates3, conv_w.astype(F32), vec(conv_b), vec(cln_g), vec(cln_b), w_pw2.astype(BF16),
      vec(b_pw2))


def _outproj_kernel(ya_ref, yb_ref, wa_ref, wb_ref, x_ref, g_ref, o_ref, *, tn, n_steps):
    j = pl.program_id(1)
    acc = jnp.dot(ya_ref[...], wa_ref[...], preferred_element_type=F32)
    acc = acc + jnp.dot(yb_ref[...], wb_ref[...], preferred_element_type=F32)
    o_ref[:, pl.ds(pl.multiple_of(j * tn, tn), tn)] = acc

    @pl.when(j == n_steps - 1)
    def _():
        out = o_ref[...]
        ms = jnp.mean(out * out, axis=-1, keepdims=True)
        o_ref[...] = x_ref[...] + out * lax.rsqrt(ms + NORM_EPS) * g_ref[...]


def _outproj(ya, yb, wa, wb, x2, g, tm=512, tn=512):
    m, ka = ya.shape
    d = wa.shape[1]
    kern = functools.partial(_outproj_kernel, tn=tn, n_steps=d // tn)
    return pl.pallas_call(
        kern,
        out_shape=jax.ShapeDtypeStruct((m, d), F32),
        grid=(m // tm, d // tn),
        in_specs=[pl.BlockSpec((tm, ka), lambda i, j: (i, 0)),
                  pl.BlockSpec((tm, yb.shape[1]), lambda i, j: (i, 0)),
                  pl.BlockSpec((ka, tn), lambda i, j: (0, j)),
                  pl.BlockSpec((yb.shape[1], tn), lambda i, j: (0, j)),
                  pl.BlockSpec((tm, d), lambda i, j: (i, 0)),
                  pl.BlockSpec((1, d), lambda i, j: (0, 0))],
        out_specs=pl.BlockSpec((tm, d), lambda i, j: (i, 0)),
        compiler_params=_cparams(("parallel", "arbitrary")),
        name="outproj_norm",
    )(ya, yb, wa, wb, x2, g.reshape(1, d).astype(F32))


def kernel(x, norm_pre_g, w_in, mu_shift, w0, w_lora_up, a0, a_lora_up, k_k, k_a, r_k, lnx_g,
           lnx_b, conv_w, conv_b, cln_g, cln_b, w_pw2, b_pw2, w_out, norm_post_g):
    bsz, seq, d_model = x.shape
    rwkv_w = w0.shape[0]
    conv_wd = conv_b.shape[0]
    shift_cols = 3 * rwkv_w + LORA_W + LORA_A
    c1 = shift_cols + rwkv_w
    c2 = c1 + conv_wd
    c3 = c2 + conv_wd
    m = bsz * seq
    x2 = x.reshape(m, d_model)

    pad = LORA_PAD - (LORA_W + LORA_A)
    w_shift = jnp.concatenate([w_in[:, :shift_cols], jnp.zeros((d_model, pad), w_in.dtype)],
                              axis=1).astype(BF16)
    mu_p = jnp.concatenate([mu_shift, jnp.zeros((pad,), mu_shift.dtype)]).reshape(1, -1).astype(F32)
    w_gates = jnp.concatenate([w_in[:, shift_cols:c1], w_in[:, c3:]], axis=1).astype(BF16)
    w_gv = w_in[:, c1:c2].astype(BF16)
    w_gg = w_in[:, c2:c3].astype(BF16)
    wup = jnp.zeros((LORA_PAD, rwkv_w), F32).at[:LORA_W].set(w_lora_up.astype(F32))
    aup = jnp.zeros((LORA_PAD, rwkv_w), F32).at[LORA_W:LORA_W + LORA_A].set(a_lora_up.astype(F32))

    h = _prenorm(x2, norm_pre_g.astype(F32))
    p = _mm_shift(h, w_shift, mu_p, seq)
    gates = _mm_silu(h, w_gates)
    u = _mm_glu(h, w_gv, w_gg)

    p3 = p.reshape(bsz, seq, -1)
    gates3 = gates.reshape(bsz, seq, -1)
    y_rwkv = _rwkv(p3, gates3, w0, a0, k_k, k_a, r_k, lnx_g, lnx_b, wup, aup, rwkv_w)
    y_conv = _conv(u.reshape(bsz, seq, conv_wd), gates3, conv_w, conv_b, cln_g, cln_b, w_pw2,
                   b_pw2, gate_col_blk=rwkv_w // conv_wd)

    w_out_b = w_out.astype(BF16)
    out = _outproj(y_rwkv.reshape(m, rwkv_w), y_conv.reshape(m, conv_wd),
                   w_out_b[:rwkv_w], w_out_b[rwkv_w:], x2, norm_post_g)
    return out.reshape(bsz, seq, d_model)
```

```python
import functools

import jax
import jax.numpy as jnp
from jax import lax
from jax.experimental import pallas as pl
from jax.experimental.pallas import tpu as pltpu

F32 = jnp.float32
BF16 = jnp.bfloat16

HEAD_SIZE = 64
LORA_W = 96
LORA_A = 96
CONV_K = 31
NORM_EPS = 1e-6
LN_EPS = 1e-5
GN_EPS = 1e-5 * HEAD_SIZE

LANES = 128
LORA_PAD = 256
CHUNK = 64
VMEM_LIMIT = 56 * 1024 * 1024

HI = lax.Precision.HIGHEST


def _cparams(sem):
    return pltpu.CompilerParams(dimension_semantics=sem, vmem_limit_bytes=VMEM_LIMIT)


def _prenorm_kernel(x_ref, g_ref, o_ref):
    x = x_ref[...]
    ms = jnp.mean(x * x, axis=-1, keepdims=True)
    o_ref[...] = (x * lax.rsqrt(ms + NORM_EPS) * g_ref[...]).astype(o_ref.dtype)


def _prenorm(x2, g, tm=256):
    m, d = x2.shape
    return pl.pallas_call(
        _prenorm_kernel,
        out_shape=jax.ShapeDtypeStruct((m, d), BF16),
        grid=(m // tm,),
        in_specs=[pl.BlockSpec((tm, d), lambda i: (i, 0)),
                  pl.BlockSpec((1, d), lambda i: (0, 0))],
        out_specs=pl.BlockSpec((tm, d), lambda i: (i, 0)),
        compiler_params=_cparams(("parallel",)),
        name="prenorm",
    )(x2, g.reshape(1, d))


def _mm_shift_kernel(h_ref, w_ref, mu_ref, o_ref, carry_ref, *, tiles_per_seq):
    i = pl.program_id(1)
    p = jnp.dot(h_ref[...], w_ref[...], preferred_element_type=F32)

    @pl.when(i % tiles_per_seq == 0)
    def _():
        carry_ref[...] = jnp.zeros_like(carry_ref)

    rows = lax.broadcasted_iota(jnp.int32, p.shape, 0)
    prev = jnp.where(rows == 0, carry_ref[...], pltpu.roll(p, 1, axis=0))
    carry_ref[...] = p[-1:, :]
    o_ref[...] = p + (prev - p) * mu_ref[...]


def _mm_shift(h, w, mu, seq_len, tm=1024, tn=640):
    m, k = h.shape
    n = w.shape[1]
    tm = min(tm, seq_len)
    kern = functools.partial(_mm_shift_kernel, tiles_per_seq=seq_len // tm)
    return pl.pallas_call(
        kern,
        out_shape=jax.ShapeDtypeStruct((m, n), F32),
        grid=(n // tn, m // tm),
        in_specs=[pl.BlockSpec((tm, k), lambda j, i: (i, 0)),
                  pl.BlockSpec((k, tn), lambda j, i: (0, j)),
                  pl.BlockSpec((1, tn), lambda j, i: (0, j))],
        out_specs=pl.BlockSpec((tm, tn), lambda j, i: (i, j)),
        scratch_shapes=[pltpu.VMEM((1, tn), F32)],
        compiler_params=_cparams(("parallel", "arbitrary")),
        name="inproj_shift",
    )(h, w, mu)


def _mm_silu_kernel(h_ref, w_ref, o_ref):
    p = jnp.dot(h_ref[...], w_ref[...], preferred_element_type=F32)
    o_ref[...] = p * jax.nn.sigmoid(p)


def _mm_silu(h, w, tm=1024, tn=512):
    m, k = h.shape
    n = w.shape[1]
    tm = min(tm, m)
    return pl.pallas_call(
        _mm_silu_kernel,
        out_shape=jax.ShapeDtypeStruct((m, n), F32),
        grid=(m // tm, n // tn),
        in_specs=[pl.BlockSpec((tm, k), lambda i, j: (i, 0)),
                  pl.BlockSpec((k, tn), lambda i, j: (0, j))],
        out_specs=pl.BlockSpec((tm, tn), lambda i, j: (i, j)),
        compiler_params=_cparams(("parallel", "arbitrary")),
        name="inproj_silu",
    )(h, w)


def _mm_glu_kernel(h_ref, wv_ref, wg_ref, o_ref):
    h = h_ref[...]
    pv = jnp.dot(h, wv_ref[...], preferred_element_type=F32)
    pg = jnp.dot(h, wg_ref[...], preferred_element_type=F32)
    o_ref[...] = pv * jax.nn.sigmoid(pg)


def _mm_glu(h, wv, wg, tm=1024, tn=256):
    m, k = h.shape
    n = wv.shape[1]
    tm = min(tm, m)
    return pl.pallas_call(
        _mm_glu_kernel,
        out_shape=jax.ShapeDtypeStruct((m, n), F32),
        grid=(m // tm, n // tn),
        in_specs=[pl.BlockSpec((tm, k), lambda i, j: (i, 0)),
                  pl.BlockSpec((k, tn), lambda i, j: (0, j)),
                  pl.BlockSpec((k, tn), lambda i, j: (0, j))],
        out_specs=pl.BlockSpec((tm, tn), lambda i, j: (i, j)),
        compiler_params=_cparams(("parallel", "arbitrary")),
        name="inproj_glu",
    )(h, wv, wg)


def _bf(z):
    return z.astype(BF16)


def _mm(a, b):
    return jnp.dot(a, b, preferred_element_type=F32)


def _mm_nt(a, b):
    return lax.dot_general(a, b, (((1,), (1,)), ((), ())), preferred_element_type=F32)


def _split3(z):
    hi = _bf(z)
    r1 = z - hi.astype(F32)
    mid = _bf(r1)
    lo = _bf(r1 - mid.astype(F32))
    return hi, mid, lo


def _mm_exact_lhs(a_bf, z):
    hi, mid, lo = _split3(z)
    return _mm(a_bf, hi) + _mm(a_bf, mid) + _mm(a_bf, lo)


def _mm_x3(a, b):
    a_hi = _bf(a)
    a_lo = _bf(a - a_hi.astype(F32))
    b_hi = _bf(b)
    b_lo = _bf(b - b_hi.astype(F32))
    return _mm(a_hi, b_hi) + _mm(a_hi, b_lo) + _mm(a_lo, b_hi)


def _stack(z, head0):
    zb = _bf(z)
    zero = jnp.zeros_like(zb)
    return jnp.concatenate([jnp.where(head0, zb, zero), jnp.where(head0, zero, zb)], axis=0)


def _unit_lower_inverse(a_list, rows, cols, head0):
    eye = (rows == cols).astype(F32)
    same8 = (rows // 8) == (cols // 8)
    a8 = [jnp.where(same8, a, 0.0) for a in a_list]
    a8_s = [_stack(z, head0) for z in a8]
    a8_2 = [_mm(_bf(z), zs) for z, zs in zip(a8, a8_s)]
    a8_2s = [_stack(z, head0) for z in a8_2]
    a8_4s = [_stack(_mm(_bf(z), zs), head0) for z, zs in zip(a8_2, a8_2s)]
    p1 = [eye + z for z in a8]
    p1 = [p + _mm(_bf(p), zs) for p, zs in zip(p1, a8_2s)]
    t = [p + _mm(_bf(p), zs) for p, zs in zip(p1, a8_4s)]
    blk = 8
    while blk < CHUNK:
        inner = (rows // blk) == (cols // blk)
        outer = (rows // (2 * blk)) == (cols // (2 * blk))
        join = outer & jnp.logical_not(inner)
        e_s = [_stack(jnp.where(join, a, 0.0), head0) for a in a_list]
        t_b = [_bf(z) for z in t]
        t_s = [_stack(z, head0) for z in t]
        te = [_bf(_mm(tb, es)) for tb, es in zip(t_b, e_s)]
        t = [z + _mm(tez, ts) for z, tez, ts in zip(t, te, t_s)]
        blk *= 2
    return t


def _rwkv_kernel(r_ref, k_ref, v_ref, lora_ref, gate_ref, w0_ref, a0_ref, kk_ref, ka_ref,
                 rk_ref, lng_ref, lnb_ref, wup_ref, aup_ref, o_ref, s_ref, y_ref, *, n_chunks):
    @pl.when(pl.program_id(2) == 0)
    def _():
        s_ref[...] = jnp.zeros_like(s_ref)

    lane = lax.broadcasted_iota(jnp.int32, (1, LANES), 1)
    head0 = lane < HEAD_SIZE
    rows = lax.broadcasted_iota(jnp.int32, (CHUNK, LANES), 0)
    cols = lax.broadcasted_iota(jnp.int32, (CHUNK, LANES), 1) % HEAD_SIZE
    strict = cols < rows
    incl = cols <= rows
    cr = lax.broadcasted_iota(jnp.int32, (CHUNK, CHUNK), 0)
    cc = lax.broadcasted_iota(jnp.int32, (CHUNK, CHUNK), 1)
    cum_l = _bf((cc <= cr).astype(F32))

    def seg_sum(z):
        s0 = jnp.sum(jnp.where(head0, z, 0.0), axis=-1, keepdims=True)
        s1 = jnp.sum(jnp.where(head0, 0.0, z), axis=-1, keepdims=True)
        return jnp.where(head0, s0, s1)

    def stack(z):
        return _stack(z, head0)

    def pair_transpose(z):
        zero = jnp.zeros_like(z)
        bd = jnp.concatenate([jnp.where(head0, z, zero), jnp.where(head0, zero, z)], axis=0)
        bd_t = bd.T
        return bd_t[:CHUNK] + bd_t[CHUNK:]

    r = r_ref[...]
    k = k_ref[...]
    v = v_ref[...]
    lo = lora_ref[...]
    zw = w0_ref[...] + _mm_x3(jnp.tanh(lo), wup_ref[...])
    nz = -zw
    softplus = jnp.maximum(nz, 0.0) + jnp.log(1.0 + jnp.exp(-jnp.abs(nz)))
    w_log = -softplus - 0.5
    lw = -jnp.exp(w_log)
    alr = jax.nn.sigmoid(a0_ref[...] + _mm_x3(lo, aup_ref[...]))
    kkr = k * kk_ref[...]
    nrm = jnp.sqrt(seg_sum(kkr * kkr))
    kk = kkr / jnp.maximum(nrm, 1e-12)
    kh = k * (1.0 + (alr - 1.0) * ka_ref[...])
    bb = kk * alr
    aa = -kk
    bonus = seg_sum(r * kh * rk_ref[...]) * v

    chunks = range(n_chunks)
    sls = [slice(c * CHUNK, (c + 1) * CHUNK) for c in chunks]
    g = [_mm_exact_lhs(cum_l, lw[sl]) for sl in sls]
    g_end = [z[CHUNK - 1:CHUNK, :] for z in g]
    e_pos = [jnp.exp(z) for z in g]
    e_neg = [jnp.exp(-z) for z in g]
    e_tail = [jnp.exp(ze - z) for z, ze in zip(g, g_end)]
    r_t = [_bf(r[sl] * e) for sl, e in zip(sls, e_pos)]
    a_t = [_bf(aa[sl] * jnp.exp(z - lw[sl])) for sl, z in zip(sls, g)]
    b_ts = [stack(bb[sl] * e) for sl, e in zip(sls, e_neg)]
    k_ts = [stack(kh[sl] * e) for sl, e in zip(sls, e_neg)]
    b_hs = [stack(bb[sl] * e) for sl, e in zip(sls, e_tail)]
    k_hs = [stack(kh[sl] * e) for sl, e in zip(sls, e_tail)]
    v_s = [stack(v[sl]) for sl in sls]
    v_t = [_bf(pair_transpose(v[sl])) for sl in sls]
    decay = [jnp.exp(z) for z in g_end]

    gram = [_mm_nt(jnp.concatenate([a, rr], axis=0), jnp.concatenate([b, kx], axis=0))
            for a, rr, b, kx in zip(a_t, r_t, b_ts, k_ts)]
    a_ab = [jnp.where(strict, z[:CHUNK, :LANES], 0.0) for z in gram]
    a_ak = [_bf(jnp.where(strict, z[:CHUNK, LANES:], 0.0)) for z in gram]
    a_r = [_bf(jnp.where(jnp.concatenate([incl, incl], axis=1), z[CHUNK:], 0.0)) for z in gram]

    t_inv = [_bf(z) for z in _unit_lower_inverse(a_ab, rows, cols, head0)]
    akv = [_mm(a, vs) for a, vs in zip(a_ak, v_s)]
    w12 = [_mm(t, jnp.concatenate([stack(a), stack(z)], axis=1))
           for t, a, z in zip(t_inv, a_t, akv)]
    w1 = [z[:, :LANES] for z in w12]
    w2 = [z[:, LANES:] for z in w12]
    w1_b = [_bf(z) for z in w1]
    w1_t = [_bf(pair_transpose(z)) for z in w1]
    w2_t = [_bf(pair_transpose(z)) for z in w2]
    m_cs = [stack(_mm(wt, b)) for wt, b in zip(w1_t, b_hs)]
    n_c = [_mm(jnp.concatenate([wt, vx], axis=1), jnp.concatenate([b, kx], axis=0))
           for wt, vx, b, kx in zip(w2_t, v_t, b_hs, k_hs)]

    s = s_ref[...]
    us_prev = None
    for c in range(n_chunks + 1):
        if c < n_chunks:
            s_b = _bf(s)
            s_s = stack(s)
            s = s * decay[c] + _mm(s_b, m_cs[c]) + n_c[c]
            us = _mm_nt(jnp.concatenate([w1_b[c], r_t[c]], axis=0), s_s)
        if c > 0:
            p = c - 1
            u = us_prev[:CHUNK] + w2[p]
            y_ref[sls[p], :] = us_prev[CHUNK:] + _mm(
                a_r[p], jnp.concatenate([stack(u), v_s[p]], axis=0))
        us_prev = us
    s_ref[...] = s

    y = y_ref[...]
    inv_n = 1.0 / HEAD_SIZE
    mu = seg_sum(y) * inv_n
    d = y - mu
    var = seg_sum(d * d) * inv_n
    yn = d * lax.rsqrt(var + GN_EPS) * lng_ref[...] + lnb_ref[...]
    o_ref[...] = ((yn + bonus) * gate_ref[...]).astype(o_ref.dtype)


def _rwkv(p3, gates3, w0, a0, k_k, k_a, r_k, lnx_g, lnx_b, wup, aup, rwkv_w, tt=512):
    b, t, _ = p3.shape
    n_pairs = rwkv_w // LANES
    lora_blk = (3 * rwkv_w) // LORA_PAD
    vec = lambda z: z.reshape(1, rwkv_w).astype(F32)
    col = lambda off: pl.BlockSpec((None, tt, LANES), lambda bi, hp, ti: (bi, ti, off + hp))
    par = pl.BlockSpec((1, LANES), lambda bi, hp, ti: (0, hp))
    up = pl.BlockSpec((LORA_PAD, LANES), lambda bi, hp, ti: (0, hp))
    kern = functools.partial(_rwkv_kernel, n_chunks=tt // CHUNK)
    return pl.pallas_call(
        kern,
        out_shape=jax.ShapeDtypeStruct((b, t, rwkv_w), BF16),
        grid=(b, n_pairs, t // tt),
        in_specs=[col(0), col(n_pairs), col(2 * n_pairs),
                  pl.BlockSpec((None, tt, LORA_PAD), lambda bi, hp, ti: (bi, ti, lora_blk)),
                  pl.BlockSpec((None, tt, LANES), lambda bi, hp, ti: (bi, ti, hp)),
                  par, par, par, par, par, par, par, up, up],
        out_specs=pl.BlockSpec((None, tt, LANES), lambda bi, hp, ti: (bi, ti, hp)),
        scratch_shapes=[pltpu.VMEM((CHUNK, LANES), F32), pltpu.VMEM((tt, LANES), F32)],
        compiler_params=_cparams(("parallel", "parallel", "arbitrary")),
        name="rwkv7",
    )(p3, p3, p3, p3, gates3, vec(w0), vec(a0), vec(k_k), vec(k_a), vec(r_k), vec(lnx_g),
      vec(lnx_b), wup, aup)


HALO = 32


def _conv_kernel(u_ref, gate_ref, cw_ref, cb_ref, lg_ref, lb_ref, w_ref, b_ref, o_ref, buf_ref,
                 *, tt):
    @pl.when(pl.program_id(1) == 0)
    def _():
        buf_ref[0:HALO, :] = jnp.zeros((HALO, buf_ref.shape[1]), F32)

    buf_ref[HALO:HALO + tt, :] = u_ref[...]
    base = HALO - (CONV_K - 1)
    acc = jnp.zeros((tt, buf_ref.shape[1]), F32) + cb_ref[...]
    for j in range(CONV_K):
        acc = acc + buf_ref[base + j:base + j + tt, :] * cw_ref[j:j + 1, :]
    tail = buf_ref[tt:tt + HALO, :]
    buf_ref[0:HALO, :] = tail

    mu = jnp.mean(acc, axis=-1, keepdims=True)
    d = acc - mu
    var = jnp.mean(d * d, axis=-1, keepdims=True)
    c = d * lax.rsqrt(var + LN_EPS) * lg_ref[...] + lb_ref[...]
    c = c * jax.nn.sigmoid(c)
    z = jnp.dot(c.astype(BF16), w_ref[...], preferred_element_type=F32) + b_ref[...]
    o_ref[...] = (z * gate_ref[...]).astype(o_ref.dtype)


def _conv(u3, gates3, conv_w, conv_b, cln_g, cln_b, w_pw2, b_pw2, gate_col_blk, tt=256):
    b, t, cw = u3.shape
    vec = lambda z: z.reshape(1, cw).astype(F32)
    par = pl.BlockSpec((1, cw), lambda bi, ti: (0, 0))
    kern = functools.partial(_conv_kernel, tt=tt)
    return pl.pallas_call(
        kern,
        out_shape=jax.ShapeDtypeStruct((b, t, cw), BF16),
        grid=(b, t // tt),
        in_specs=[pl.BlockSpec((None, tt, cw), lambda bi, ti: (bi, ti, 0)),
                  pl.BlockSpec((None, tt, cw), lambda bi, ti: (bi, ti, gate_col_blk)),
                  pl.BlockSpec((CONV_K, cw), lambda bi, ti: (0, 0)),
                  par, par, par,
                  pl.BlockSpec((cw, cw), lambda bi, ti: (0, 0)),
                  par],
        out_specs=pl.BlockSpec((None, tt, cw), lambda bi, ti: (bi, ti, 0)),
        scratch_shapes=[pltpu.VMEM((HALO + tt, cw), F32)],
        compiler_params=_cparams(("parallel", "arbitrary")),
        name="conv_group",
    )(u3, gates3, conv_w.astype(F32), vec(conv_b), vec(cln_g), vec(cln_b), w_pw2.astype(BF16),
      vec(b_pw2))


def _outproj_kernel(ya_ref, yb_ref, wa_ref, wb_ref, x_ref, g_ref, o_ref, *, tn, n_steps):
    j = pl.program_id(1)
    acc = jnp.dot(ya_ref[...], wa_ref[...], preferred_element_type=F32)
    acc = acc + jnp.dot(yb_ref[...], wb_ref[...], preferred_element_type=F32)
    o_ref[:, pl.ds(pl.multiple_of(j * tn, tn), tn)] = acc

    @pl.when(j == n_steps - 1)
    def _():
        out = o_ref[...]
        ms = jnp.mean(out * out, axis=-1, keepdims=True)
        o_ref[...] = x_ref[...] + out * lax.rsqrt(ms + NORM_EPS) * g_ref[...]


def _outproj(ya, yb, wa, wb, x2, g, tm=512, tn=512):
    m, ka = ya.shape
    d = wa.shape[1]
    kern = functools.partial(_outproj_kernel, tn=tn, n_steps=d // tn)
    return pl.pallas_call(
        kern,
        out_shape=jax.ShapeDtypeStruct((m, d), F32),
        grid=(m // tm, d // tn),
        in_specs=[pl.BlockSpec((tm, ka), lambda i, j: (i, 0)),
                  pl.BlockSpec((tm, yb.shape[1]), lambda i, j: (i, 0)),
                  pl.BlockSpec((ka, tn), lambda i, j: (0, j)),
                  pl.BlockSpec((yb.shape[1], tn), lambda i, j: (0, j)),
                  pl.BlockSpec((tm, d), lambda i, j: (i, 0)),
                  pl.BlockSpec((1, d), lambda i, j: (0, 0))],
        out_specs=pl.BlockSpec((tm, d), lambda i, j: (i, 0)),
        compiler_params=_cparams(("parallel", "arbitrary")),
        name="outproj_norm",
    )(ya, yb, wa, wb, x2, g.reshape(1, d).astype(F32))


def kernel(x, norm_pre_g, w_in, mu_shift, w0, w_lora_up, a0, a_lora_up, k_k, k_a, r_k, lnx_g,
           lnx_b, conv_w, conv_b, cln_g, cln_b, w_pw2, b_pw2, w_out, norm_post_g):
    bsz, seq, d_model = x.shape
    rwkv_w = w0.shape[0]
    conv_wd = conv_b.shape[0]
    shift_cols = 3 * rwkv_w + LORA_W + LORA_A
    c1 = shift_cols + rwkv_w
    c2 = c1 + conv_wd
    c3 = c2 + conv_wd
    m = bsz * seq
    x2 = x.reshape(m, d_model)

    pad = LORA_PAD - (LORA_W + LORA_A)
    w_shift = jnp.concatenate([w_in[:, :shift_cols], jnp.zeros((d_model, pad), w_in.dtype)],
                              axis=1).astype(BF16)
    mu_p = jnp.concatenate([mu_shift, jnp.zeros((pad,), mu_shift.dtype)]).reshape(1, -1).astype(F32)
    w_gates = jnp.concatenate([w_in[:, shift_cols:c1], w_in[:, c3:]], axis=1).astype(BF16)
    w_gv = w_in[:, c1:c2].astype(BF16)
    w_gg = w_in[:, c2:c3].astype(BF16)
    wup = jnp.zeros((LORA_PAD, rwkv_w), F32).at[:LORA_W].set(w_lora_up.astype(F32))
    aup = jnp.zeros((LORA_PAD, rwkv_w), F32).at[LORA_W:LORA_W + LORA_A].set(a_lora_up.astype(F32))

    h = _prenorm(x2, norm_pre_g.astype(F32))
    p = _mm_shift(h, w_shift, mu_p, seq)
    gates = _mm_silu(h, w_gates)
    u = _mm_glu(h, w_gv, w_gg)

    p3 = p.reshape(bsz, seq, -1)
    gates3 = gates.reshape(bsz, seq, -1)
    y_rwkv = _rwkv(p3, gates3, w0, a0, k_k, k_a, r_k, lnx_g, lnx_b, wup, aup, rwkv_w)
    y_conv = _conv(u.reshape(bsz, seq, conv_wd), gates3, conv_w, conv_b, cln_g, cln_b, w_pw2,
                   b_pw2, gate_col_blk=rwkv_w // conv_wd)

    w_out_b = w_out.astype(BF16)
    out = _outproj(y_rwkv.reshape(m, rwkv_w), y_conv.reshape(m, conv_wd),
                   w_out_b[:rwkv_w], w_out_b[rwkv_w:], x2, norm_post_g)
    return out.reshape(bsz, seq, d_model)
```

```python
import functools

import jax
import jax.numpy as jnp
from jax import lax
from jax.experimental import pallas as pl
from jax.experimental.pallas import tpu as pltpu

F32 = jnp.float32
BF16 = jnp.bfloat16

HEAD_SIZE = 64
LORA_W = 96
LORA_A = 96
CONV_K = 31
NORM_EPS = 1e-6
LN_EPS = 1e-5
GN_EPS = 1e-5 * HEAD_SIZE

LANES = 128
SUBLANES = 8
LORA_PAD = 256
CHUNK = 64
PAIRS_PER_STEP = 2
VMEM_LIMIT = 56 * 1024 * 1024

assert CHUNK == HEAD_SIZE


def _cparams(sem):
    return pltpu.CompilerParams(dimension_semantics=sem, vmem_limit_bytes=VMEM_LIMIT)


def _prenorm_kernel(x_ref, g_ref, o_ref):
    x = x_ref[...]
    ms = jnp.mean(x * x, axis=-1, keepdims=True)
    o_ref[...] = (x * lax.rsqrt(ms + NORM_EPS) * g_ref[...]).astype(o_ref.dtype)


def _prenorm(x2, g, tm=256):
    m, d = x2.shape
    return pl.pallas_call(
        _prenorm_kernel,
        out_shape=jax.ShapeDtypeStruct((m, d), BF16),
        grid=(m // tm,),
        in_specs=[pl.BlockSpec((tm, d), lambda i: (i, 0)),
                  pl.BlockSpec((1, d), lambda i: (0, 0))],
        out_specs=pl.BlockSpec((tm, d), lambda i: (i, 0)),
        compiler_params=_cparams(("parallel",)),
        name="prenorm",
    )(x2, g.reshape(1, d))


def _mm_shift_kernel(h_ref, w_ref, mu_ref, o_ref, carry_ref, *, tiles_per_seq):
    i = pl.program_id(1)
    p = jnp.dot(h_ref[...], w_ref[...], preferred_element_type=F32)

    @pl.when(i % tiles_per_seq == 0)
    def _():
        carry_ref[...] = jnp.zeros_like(carry_ref)

    rows = lax.broadcasted_iota(jnp.int32, p.shape, 0)
    prev = jnp.where(rows == 0, carry_ref[...], pltpu.roll(p, 1, axis=0))
    carry_ref[...] = p[-1:, :]
    o_ref[...] = p + (prev - p) * mu_ref[...]


def _mm_shift(h, w, mu, n, seq_len, tm=512, tn=1280):
    m, k = h.shape
    tm = min(tm, seq_len)
    kern = functools.partial(_mm_shift_kernel, tiles_per_seq=seq_len // tm)
    return pl.pallas_call(
        kern,
        out_shape=jax.ShapeDtypeStruct((m, n), F32),
        grid=(n // tn, m // tm),
        in_specs=[pl.BlockSpec((tm, k), lambda j, i: (i, 0)),
                  pl.BlockSpec((k, tn), lambda j, i: (0, j)),
                  pl.BlockSpec((1, tn), lambda j, i: (0, j))],
        out_specs=pl.BlockSpec((tm, tn), lambda j, i: (i, j)),
        scratch_shapes=[pltpu.VMEM((1, tn), F32)],
        compiler_params=_cparams(("parallel", "arbitrary")),
        name="inproj_shift",
    )(h, w, mu)


def _mm_gates_kernel(h_ref, wgr_ref, wgv_ref, wgg_ref, wgc_ref, gr_ref, gc_ref, u_ref):
    h = h_ref[...]
    dot = lambda w_ref: jnp.dot(h, w_ref[...], preferred_element_type=F32)
    pr = dot(wgr_ref)
    gr_ref[...] = pr * jax.nn.sigmoid(pr)
    pc = dot(wgc_ref)
    gc_ref[...] = pc * jax.nn.sigmoid(pc)
    u_ref[...] = dot(wgv_ref) * jax.nn.sigmoid(dot(wgg_ref))


def _mm_gates(h, w, col0, width, tm=1024, tn=256):
    m, k = h.shape
    tm = min(tm, m)
    blk0 = col0 // tn
    per = width // tn
    wspec = lambda g: pl.BlockSpec((k, tn), lambda i, j: (0, blk0 + g * per + j))
    ospec = pl.BlockSpec((tm, tn), lambda i, j: (i, j))
    oshape = jax.ShapeDtypeStruct((m, width), F32)
    return pl.pallas_call(
        _mm_gates_kernel,
        out_shape=(oshape, oshape, oshape),
        grid=(m // tm, per),
        in_specs=[pl.BlockSpec((tm, k), lambda i, j: (i, 0)), wspec(0), wspec(1), wspec(2), wspec(3)],
        out_specs=(ospec, ospec, ospec),
        compiler_params=_cparams(("parallel", "arbitrary")),
        name="inproj_gates",
    )(h, w, w, w, w)


def _bf(z):
    return z.astype(BF16)


def _mm(a, b):
    return jnp.dot(a, b, preferred_element_type=F32)


def _mm_nt(a, b):
    return lax.dot_general(a, b, (((1,), (1,)), ((), ())), preferred_element_type=F32)


def _cumsum_rows(cum_l, z):
    hi = _bf(z)
    r1 = z - hi.astype(F32)
    mid = _bf(r1)
    lo = _bf(r1 - mid.astype(F32))
    two = _mm(cum_l, jnp.concatenate([hi, mid], axis=1))
    return two[:, :LANES] + two[:, LANES:] + _mm(cum_l, lo)


def _stack(z, head0):
    zb = _bf(z)
    zero = jnp.zeros_like(zb)
    return jnp.concatenate([jnp.where(head0, zb, zero), jnp.where(head0, zero, zb)], axis=0)


def _unit_lower_inverse(a_list, rows, cols, head0):
    eye = (rows == cols).astype(F32)
    same8 = (rows // 8) == (cols // 8)
    a8 = [jnp.where(same8, a, 0.0) for a in a_list]
    a8_s = [_stack(z, head0) for z in a8]
    a8_2 = [_mm(_bf(z), zs) for z, zs in zip(a8, a8_s)]
    a8_2s = [_stack(z, head0) for z in a8_2]
    a8_4s = [_stack(_mm(_bf(z), zs), head0) for z, zs in zip(a8_2, a8_2s)]
    p1 = [eye + z for z in a8]
    p1 = [p + _mm(_bf(p), zs) for p, zs in zip(p1, a8_2s)]
    t = [p + _mm(_bf(p), zs) for p, zs in zip(p1, a8_4s)]
    blk = 8
    while blk < CHUNK:
        inner = (rows // blk) == (cols // blk)
        outer = (rows // (2 * blk)) == (cols // (2 * blk))
        join = outer & jnp.logical_not(inner)
        e_s = [_stack(jnp.where(join, a, 0.0), head0) for a in a_list]
        t_b = [_bf(z) for z in t]
        t_s = [_stack(z, head0) for z in t]
        te = [_bf(_mm(tb, es)) for tb, es in zip(t_b, e_s)]
        t = [z + _mm(tez, ts) for z, tez, ts in zip(t, te, t_s)]
        blk *= 2
    return t


def _rwkv_kernel(r_ref, k_ref, v_ref, lora_ref, gate_ref, w0_ref, a0_ref, kk_ref, ka_ref,
                 rk_ref, lng_ref, lnb_ref, wup_ref, aup_ref, o_ref, s_ref, y_ref, *, n_chunks,
                 n_pairs):
    @pl.when(pl.program_id(2) == 0)
    def _():
        s_ref[...] = jnp.zeros_like(s_ref)

    lane = lax.broadcasted_iota(jnp.int32, (1, LANES), 1)
    head0 = lane < HEAD_SIZE
    rows = lax.broadcasted_iota(jnp.int32, (CHUNK, LANES), 0)
    cols = lax.broadcasted_iota(jnp.int32, (CHUNK, LANES), 1) % HEAD_SIZE
    strict = cols < rows
    incl = cols <= rows
    incl2 = jnp.concatenate([incl, incl], axis=1)
    cr = lax.broadcasted_iota(jnp.int32, (CHUNK, CHUNK), 0)
    cc = lax.broadcasted_iota(jnp.int32, (CHUNK, CHUNK), 1)
    cum_l = _bf((cc <= cr).astype(F32))

    def seg_sum(z):
        s0 = jnp.sum(jnp.where(head0, z, 0.0), axis=-1, keepdims=True)
        s1 = jnp.sum(jnp.where(head0, 0.0, z), axis=-1, keepdims=True)
        return jnp.where(head0, s0, s1)

    def stack(z):
        return _stack(z, head0)

    def pair_transpose(z):
        zero = jnp.zeros_like(z)
        bd = jnp.concatenate([jnp.where(head0, z, zero), jnp.where(head0, zero, z)], axis=0)
        bd_t = bd.T
        return bd_t[:CHUNK] + bd_t[CHUNK:]

    lo = lora_ref[...]
    zw = w0_ref[...] + _mm(_bf(jnp.tanh(lo)), wup_ref[...])
    nz = -zw
    softplus = jnp.maximum(nz, 0.0) + jnp.log(1.0 + jnp.exp(-jnp.abs(nz)))
    w_log = -softplus - 0.5
    lw_all = -jnp.exp(w_log)
    alr_all = jax.nn.sigmoid(a0_ref[...] + _mm(_bf(lo), aup_ref[...]))

    pairs = range(n_pairs)
    lanes_of = [slice(p * LANES, (p + 1) * LANES) for p in pairs]
    r, v, lw, kh, bb, aa, bonus = [], [], [], [], [], [], []
    for ls in lanes_of:
        r_p = r_ref[:, ls]
        k_p = k_ref[:, ls]
        v_p = v_ref[:, ls]
        alr = alr_all[:, ls]
        kkr = k_p * kk_ref[:, ls]
        kk = kkr / jnp.maximum(jnp.sqrt(seg_sum(kkr * kkr)), 1e-12)
        kh_p = k_p * (1.0 + (alr - 1.0) * ka_ref[:, ls])
        r.append(r_p)
        v.append(v_p)
        lw.append(lw_all[:, ls])
        kh.append(kh_p)
        bb.append(kk * alr)
        aa.append(-kk)
        bonus.append(seg_sum(r_p * kh_p * rk_ref[:, ls]) * v_p)

    rsl = lambda c: slice(c * CHUNK, (c + 1) * CHUNK)

    def precompute(units):
        cut = lambda arr: [arr[p][rsl(c)] for p, c in units]
        r_u, v_u, lw_u, kh_u, bb_u, aa_u = cut(r), cut(v), cut(lw), cut(kh), cut(bb), cut(aa)
        g = [_cumsum_rows(cum_l, z) for z in lw_u]
        g_end = [z[CHUNK - 1:CHUNK, :] for z in g]
        e_pos = [jnp.exp(z) for z in g]
        e_neg = [jnp.exp(-z) for z in g]
        e_tail = [jnp.exp(ze - z) for z, ze in zip(g, g_end)]
        r_t = [_bf(x * e) for x, e in zip(r_u, e_pos)]
        a_t = [_bf(x * jnp.exp(z - l)) for x, z, l in zip(aa_u, g, lw_u)]
        b_ts = [stack(x * e) for x, e in zip(bb_u, e_neg)]
        k_ts = [stack(x * e) for x, e in zip(kh_u, e_neg)]
        b_hs = [stack(x * e) for x, e in zip(bb_u, e_tail)]
        k_hs = [stack(x * e) for x, e in zip(kh_u, e_tail)]
        v_s = [stack(x) for x in v_u]
        v_t = [_bf(pair_transpose(x)) for x in v_u]
        decay = [jnp.exp(z) for z in g_end]

        gram = [_mm_nt(jnp.concatenate([a, rr], axis=0), jnp.concatenate([b, kx], axis=0))
                for a, rr, b, kx in zip(a_t, r_t, b_ts, k_ts)]
        a_ab = [jnp.where(strict, z[:CHUNK, :LANES], 0.0) for z in gram]
        a_ak = [_bf(jnp.where(strict, z[:CHUNK, LANES:], 0.0)) for z in gram]
        a_r = [_bf(jnp.where(incl2, z[CHUNK:], 0.0)) for z in gram]

        t_inv = [_bf(z) for z in _unit_lower_inverse(a_ab, rows, cols, head0)]
        akv = [_mm(a, vs) for a, vs in zip(a_ak, v_s)]
        w12 = [_mm(t, jnp.concatenate([stack(a), stack(z)], axis=1))
               for t, a, z in zip(t_inv, a_t, akv)]
        w1 = [z[:, :LANES] for z in w12]
        w2 = [z[:, LANES:] for z in w12]
        w1_b = [_bf(z) for z in w1]
        w1_t = [_bf(pair_transpose(z)) for z in w1]
        w2_t = [_bf(pair_transpose(z)) for z in w2]
        m_cs = [stack(_mm(wt, b)) for wt, b in zip(w1_t, b_hs)]
        n_c = [_mm(jnp.concatenate([wt, vx], axis=1), jnp.concatenate([b, kx], axis=0))
               for wt, vx, b, kx in zip(w2_t, v_t, b_hs, k_hs)]
        return dict(zip(units, zip(r_t, v_s, decay, a_r, w1_b, w2, m_cs, n_c)))

    pre = precompute([(p, c) for c in range(n_chunks) for p in pairs])

    s = [s_ref[p] for p in pairs]
    us_prev = [None] * n_pairs
    for c in range(n_chunks + 1):
        us_now = [None] * n_pairs
        if c < n_chunks:
            for p in pairs:
                r_t, _, decay, _, w1_b, _, m_cs, n_c = pre[(p, c)]
                s_b = _bf(s[p])
                s_s = stack(s[p])
                s[p] = s[p] * decay + _mm(s_b, m_cs) + n_c
                us_now[p] = _mm_nt(jnp.concatenate([w1_b, r_t], axis=0), s_s)
        if c > 0:
            for p in pairs:
                _, v_s, _, a_r, _, w2, _, _ = pre[(p, c - 1)]
                u = us_prev[p][:CHUNK] + w2
                y_ref[rsl(c - 1), lanes_of[p]] = us_prev[p][CHUNK:] + _mm(
                    a_r, jnp.concatenate([stack(u), v_s], axis=0))
        us_prev = us_now
    for p in pairs:
        s_ref[p] = s[p]

    inv_n = 1.0 / HEAD_SIZE
    for p, ls in zip(pairs, lanes_of):
        y = y_ref[:, ls]
        mu = seg_sum(y) * inv_n
        d = y - mu
        var = seg_sum(d * d) * inv_n
        yn = d * lax.rsqrt(var + GN_EPS) * lng_ref[:, ls] + lnb_ref[:, ls]
        o_ref[:, ls] = ((yn + bonus[p]) * gate_ref[:, ls]).astype(o_ref.dtype)


def _rwkv(p3, gate3, w0, a0, k_k, k_a, r_k, lnx_g, lnx_b, wup, aup, rwkv_w, tt=512):
    b, t, _ = p3.shape
    width = PAIRS_PER_STEP * LANES
    n_steps = rwkv_w // width
    lora_blk = (3 * rwkv_w) // LORA_PAD
    vec = lambda z: z.reshape(1, rwkv_w).astype(F32)
    col = lambda off: pl.BlockSpec((None, tt, width), lambda bi, hp, ti: (bi, ti, off + hp))
    par = pl.BlockSpec((1, width), lambda bi, hp, ti: (0, hp))
    up = pl.BlockSpec((LORA_PAD, width), lambda bi, hp, ti: (0, hp))
    kern = functools.partial(_rwkv_kernel, n_chunks=tt // CHUNK, n_pairs=PAIRS_PER_STEP)
    return pl.pallas_call(
        kern,
        out_shape=jax.ShapeDtypeStruct((b, t, rwkv_w), BF16),
        grid=(b, n_steps, t // tt),
        in_specs=[col(0), col(n_steps), col(2 * n_steps),
                  pl.BlockSpec((None, tt, LORA_PAD), lambda bi, hp, ti: (bi, ti, lora_blk)),
                  col(0),
                  par, par, par, par, par, par, par, up, up],
        out_specs=col(0),
        scratch_shapes=[pltpu.VMEM((PAIRS_PER_STEP, CHUNK, LANES), F32),
                        pltpu.VMEM((tt, width), F32)],
        compiler_params=_cparams(("parallel", "parallel", "arbitrary")),
        name="rwkv7",
    )(p3, p3, p3, p3, gate3, vec(w0), vec(a0), vec(k_k), vec(k_a), vec(r_k), vec(lnx_g),
      vec(lnx_b), wup, aup)


HALO = 32
CONV_ROWS = 64
CONV_LANES = 128


def _conv_kernel(u_ref, gate_ref, cw_ref, cb_ref, lg_ref, lb_ref, w_ref, b_ref, o_ref, buf_ref,
                 acc_ref, *, tt):
    cw_all = buf_ref.shape[1]

    @pl.when(pl.program_id(1) == 0)
    def _():
        buf_ref[0:HALO, :] = jnp.zeros((HALO, cw_all), F32)

    buf_ref[HALO:HALO + tt, :] = u_ref[...]
    base = HALO - (CONV_K - 1)

    win_rows = CONV_ROWS + HALO

    def lane_block(lb, carry):
        ls = pl.ds(pl.multiple_of(lb * CONV_LANES, CONV_LANES), CONV_LANES)
        bias = cb_ref[:, ls]
        for rb in range(tt // CONV_ROWS):
            win = buf_ref[rb * CONV_ROWS:rb * CONV_ROWS + win_rows, ls]
            acc = jnp.zeros((CONV_ROWS, CONV_LANES), F32) + bias
            for res in range(SUBLANES):
                sub = (base + res) % SUBLANES
                rot = win if sub == 0 else pltpu.roll(win, win_rows - sub, axis=0)
                for j in range(res, CONV_K, SUBLANES):
                    lo_row = base + j - sub
                    acc = acc + rot[lo_row:lo_row + CONV_ROWS] * cw_ref[pl.ds(j, 1), ls]
            acc_ref[rb * CONV_ROWS:(rb + 1) * CONV_ROWS, ls] = acc
        return carry

    lax.fori_loop(0, cw_all // CONV_LANES, lane_block, 0)
    tail = buf_ref[tt:tt + HALO, :]
    buf_ref[0:HALO, :] = tail

    acc = acc_ref[...]
    mu = jnp.mean(acc, axis=-1, keepdims=True)
    d = acc - mu
    var = jnp.mean(d * d, axis=-1, keepdims=True)
    c = d * lax.rsqrt(var + LN_EPS) * lg_ref[...] + lb_ref[...]
    c = c * jax.nn.sigmoid(c)
    z = jnp.dot(c.astype(BF16), w_ref[...], preferred_element_type=F32) + b_ref[...]
    o_ref[...] = (z * gate_ref[...]).astype(o_ref.dtype)


def _conv(u3, gate3, conv_w, conv_b, cln_g, cln_b, w_pw2, b_pw2, tt=256):
    b, t, cw = u3.shape
    vec = lambda z: z.reshape(1, cw).astype(F32)
    par = pl.BlockSpec((1, cw), lambda bi, ti: (0, 0))
    row = pl.BlockSpec((None, tt, cw), lambda bi, ti: (bi, ti, 0))
    kern = functools.partial(_conv_kernel, tt=tt)
    return pl.pallas_call(
        kern,
        out_shape=jax.ShapeDtypeStruct((b, t, cw), BF16),
        grid=(b, t // tt),
        in_specs=[row, row,
                  pl.BlockSpec((CONV_K, cw), lambda bi, ti: (0, 0)),
                  par, par, par,
                  pl.BlockSpec((cw, cw), lambda bi, ti: (0, 0)),
                  par],
        out_specs=row,
        scratch_shapes=[pltpu.VMEM((HALO + tt, cw), F32), pltpu.VMEM((tt, cw), F32)],
        compiler_params=_cparams(("parallel", "arbitrary")),
        name="conv_group",
    )(u3, gate3, conv_w.astype(F32), vec(conv_b), vec(cln_g), vec(cln_b), w_pw2.astype(BF16),
      vec(b_pw2))


def _outproj_kernel(ya_ref, yb_ref, wa_ref, wb_ref, x_ref, g_ref, o_ref, *, tn, n_steps):
    j = pl.program_id(1)
    acc = jnp.dot(ya_ref[...], wa_ref[...], preferred_element_type=F32)
    acc = acc + jnp.dot(yb_ref[...], wb_ref[...], preferred_element_type=F32)
    o_ref[:, pl.ds(pl.multiple_of(j * tn, tn), tn)] = acc

    @pl.when(j == n_steps - 1)
    def _():
        out = o_ref[...]
        ms = jnp.mean(out * out, axis=-1, keepdims=True)
        o_ref[...] = x_ref[...] + out * lax.rsqrt(ms + NORM_EPS) * g_ref[...]


def _outproj(ya, yb, w, x2, g, tm=512, tn=512):
    m, ka = ya.shape
    kb = yb.shape[1]
    d = w.shape[1]
    kern = functools.partial(_outproj_kernel, tn=tn, n_steps=d // tn)
    return pl.pallas_call(
        kern,
        out_shape=jax.ShapeDtypeStruct((m, d), F32),
        grid=(m // tm, d // tn),
        in_specs=[pl.BlockSpec((tm, ka), lambda i, j: (i, 0)),
                  pl.BlockSpec((tm, kb), lambda i, j: (i, 0)),
                  pl.BlockSpec((ka, tn), lambda i, j: (0, j)),
                  pl.BlockSpec((kb, tn), lambda i, j: (ka // kb, j)),
                  pl.BlockSpec((tm, d), lambda i, j: (i, 0)),
                  pl.BlockSpec((1, d), lambda i, j: (0, 0))],
        out_specs=pl.BlockSpec((tm, d), lambda i, j: (i, 0)),
        compiler_params=_cparams(("parallel", "arbitrary")),
        name="outproj_norm",
    )(ya, yb, w, w, x2, g.reshape(1, d).astype(F32))


def kernel(x, norm_pre_g, w_in, mu_shift, w0, w_lora_up, a0, a_lora_up, k_k, k_a, r_k, lnx_g,
           lnx_b, conv_w, conv_b, cln_g, cln_b, w_pw2, b_pw2, w_out, norm_post_g):
    bsz, seq, d_model = x.shape
    rwkv_w = w0.shape[0]
    conv_wd = conv_b.shape[0]
    assert rwkv_w == conv_wd
    shift_cols = 3 * rwkv_w + LORA_W + LORA_A
    shift_pad = 3 * rwkv_w + LORA_PAD
    m = bsz * seq
    x2 = x.reshape(m, d_model)

    pad = shift_pad - shift_cols
    w_p = jnp.concatenate([w_in[:, :shift_cols], jnp.zeros((d_model, pad), w_in.dtype),
                           w_in[:, shift_cols:]], axis=1).astype(BF16)
    mu_p = jnp.concatenate([mu_shift, jnp.zeros((pad,), mu_shift.dtype)]).reshape(1, -1).astype(F32)
    wup = jnp.zeros((LORA_PAD, rwkv_w), BF16).at[:LORA_W].set(w_lora_up.astype(BF16))
    aup = jnp.zeros((LORA_PAD, rwkv_w), BF16).at[LORA_W:LORA_W + LORA_A].set(a_lora_up.astype(BF16))

    h = _prenorm(x2, norm_pre_g.astype(F32))
    p = _mm_shift(h, w_p, mu_p, shift_pad, seq)
    gate_r, gate_c, u = _mm_gates(h, w_p, shift_pad, rwkv_w)

    to3 = lambda z: z.reshape(bsz, seq, -1)
    y_rwkv = _rwkv(to3(p), to3(gate_r), w0, a0, k_k, k_a, r_k, lnx_g, lnx_b, wup, aup, rwkv_w)
    y_conv = _conv(to3(u), to3(gate_c), conv_w, conv_b, cln_g, cln_b, w_pw2, b_pw2)

    out = _outproj(y_rwkv.reshape(m, rwkv_w), y_conv.reshape(m, conv_wd), w_out.astype(BF16), x2,
                   norm_post_g)
    return out.reshape(bsz, seq, d_model)
```

```python
import functools

import jax
import jax.numpy as jnp
from jax import lax
from jax.experimental import pallas as pl
from jax.experimental.pallas import tpu as pltpu

F32 = jnp.float32
BF16 = jnp.bfloat16

HEAD_SIZE = 64
LORA_W = 96
LORA_A = 96
CONV_K = 31
NORM_EPS = 1e-6
LN_EPS = 1e-5
GN_EPS = 1e-5 * HEAD_SIZE

LANES = 128
SUBLANES = 8
LORA_PAD = 256
CHUNK = 64
PAIRS_PER_STEP = 8
VMEM_LIMIT = 56 * 1024 * 1024

assert CHUNK == HEAD_SIZE


def _cparams(sem):
    return pltpu.CompilerParams(dimension_semantics=sem, vmem_limit_bytes=VMEM_LIMIT)


def _prenorm_kernel(x_ref, g_ref, o_ref):
    x = x_ref[...]
    ms = jnp.mean(x * x, axis=-1, keepdims=True)
    o_ref[...] = (x * lax.rsqrt(ms + NORM_EPS) * g_ref[...]).astype(o_ref.dtype)


def _prenorm(x2, g, tm=256):
    m, d = x2.shape
    return pl.pallas_call(
        _prenorm_kernel,
        out_shape=jax.ShapeDtypeStruct((m, d), BF16),
        grid=(m // tm,),
        in_specs=[pl.BlockSpec((tm, d), lambda i: (i, 0)),
                  pl.BlockSpec((1, d), lambda i: (0, 0))],
        out_specs=pl.BlockSpec((tm, d), lambda i: (i, 0)),
        compiler_params=_cparams(("parallel",)),
        name="prenorm",
    )(x2, g.reshape(1, d))


def _mm_shift_kernel(h_ref, w_ref, mu_ref, o_ref, carry_ref, *, tiles_per_seq):
    i = pl.program_id(1)
    p = jnp.dot(h_ref[...], w_ref[...], preferred_element_type=F32)

    @pl.when(i % tiles_per_seq == 0)
    def _():
        carry_ref[...] = jnp.zeros_like(carry_ref)

    rows = lax.broadcasted_iota(jnp.int32, p.shape, 0)
    prev = jnp.where(rows == 0, carry_ref[...], pltpu.roll(p, 1, axis=0))
    carry_ref[...] = p[-1:, :]
    o_ref[...] = p + (prev - p) * mu_ref[...]


def _mm_shift(h, w, mu, seq_len, tm=512, tn=1280):
    m, k = h.shape
    n = w.shape[1]
    tm = min(tm, seq_len)
    kern = functools.partial(_mm_shift_kernel, tiles_per_seq=seq_len // tm)
    return pl.pallas_call(
        kern,
        out_shape=jax.ShapeDtypeStruct((m, n), F32),
        grid=(n // tn, m // tm),
        in_specs=[pl.BlockSpec((tm, k), lambda j, i: (i, 0)),
                  pl.BlockSpec((k, tn), lambda j, i: (0, j)),
                  pl.BlockSpec((1, tn), lambda j, i: (0, j))],
        out_specs=pl.BlockSpec((tm, tn), lambda j, i: (i, j)),
        scratch_shapes=[pltpu.VMEM((1, tn), F32)],
        compiler_params=_cparams(("parallel", "arbitrary")),
        name="inproj_shift",
    )(h, w, mu)


def _mm_gates_kernel(h_ref, wgr_ref, wgv_ref, wgg_ref, wgc_ref, gr_ref, gc_ref, u_ref):
    h = h_ref[...]
    dot = lambda w_ref: jnp.dot(h, w_ref[...], preferred_element_type=F32)
    pr = dot(wgr_ref)
    gr_ref[...] = pr * jax.nn.sigmoid(pr)
    pc = dot(wgc_ref)
    gc_ref[...] = pc * jax.nn.sigmoid(pc)
    u_ref[...] = dot(wgv_ref) * jax.nn.sigmoid(dot(wgg_ref))


def _mm_gates(h, w, width, tm=1024, tn=256):
    m, k = h.shape
    tm = min(tm, m)
    per = width // tn
    wspec = lambda g: pl.BlockSpec((k, tn), lambda i, j: (0, g * per + j))
    ospec = pl.BlockSpec((tm, tn), lambda i, j: (i, j))
    oshape = jax.ShapeDtypeStruct((m, width), F32)
    return pl.pallas_call(
        _mm_gates_kernel,
        out_shape=(oshape, oshape, oshape),
        grid=(m // tm, per),
        in_specs=[pl.BlockSpec((tm, k), lambda i, j: (i, 0)), wspec(0), wspec(1), wspec(2), wspec(3)],
        out_specs=(ospec, ospec, ospec),
        compiler_params=_cparams(("parallel", "arbitrary")),
        name="inproj_gates",
    )(h, w, w, w, w)


def _bf(z):
    return z.astype(BF16)


def _mm(a, b):
    return jnp.dot(a, b, preferred_element_type=F32)


def _mm_nt(a, b):
    return lax.dot_general(a, b, (((1,), (1,)), ((), ())), preferred_element_type=F32)


def _cumsum_rows(cum_l, z):
    hi = _bf(z)
    r1 = z - hi.astype(F32)
    mid = _bf(r1)
    lo = _bf(r1 - mid.astype(F32))
    two = _mm(cum_l, jnp.concatenate([hi, mid], axis=1))
    return two[:, :LANES] + two[:, LANES:] + _mm(cum_l, lo)


def _stack(z, head0):
    zb = _bf(z)
    zero = jnp.zeros_like(zb)
    return jnp.concatenate([jnp.where(head0, zb, zero), jnp.where(head0, zero, zb)], axis=0)


def _unit_lower_inverse(a_list, rows, cols, head0):
    eye = (rows == cols).astype(F32)
    same8 = (rows // 8) == (cols // 8)
    a8 = [jnp.where(same8, a, 0.0) for a in a_list]
    a8_s = [_stack(z, head0) for z in a8]
    a8_2 = [_mm(_bf(z), zs) for z, zs in zip(a8, a8_s)]
    a8_2s = [_stack(z, head0) for z in a8_2]
    a8_4s = [_stack(_mm(_bf(z), zs), head0) for z, zs in zip(a8_2, a8_2s)]
    p1 = [eye + z for z in a8]
    p1 = [p + _mm(_bf(p), zs) for p, zs in zip(p1, a8_2s)]
    t = [p + _mm(_bf(p), zs) for p, zs in zip(p1, a8_4s)]
    blk = 8
    while blk < CHUNK:
        inner = (rows // blk) == (cols // blk)
        outer = (rows // (2 * blk)) == (cols // (2 * blk))
        join = outer & jnp.logical_not(inner)
        e_s = [_stack(jnp.where(join, a, 0.0), head0) for a in a_list]
        t_b = [_bf(z) for z in t]
        t_s = [_stack(z, head0) for z in t]
        te = [_bf(_mm(tb, es)) for tb, es in zip(t_b, e_s)]
        t = [z + _mm(tez, ts) for z, tez, ts in zip(t, te, t_s)]
        blk *= 2
    return t


def _rwkv_kernel(r_ref, k_ref, v_ref, lora_ref, gate_ref, w0_ref, a0_ref, kk_ref, ka_ref,
                 rk_ref, lng_ref, lnb_ref, wup_ref, aup_ref, o_ref, s_ref, y_ref, *, n_chunks,
                 n_pairs):
    @pl.when(pl.program_id(2) == 0)
    def _():
        s_ref[...] = jnp.zeros_like(s_ref)

    lane = lax.broadcasted_iota(jnp.int32, (1, LANES), 1)
    head0 = lane < HEAD_SIZE
    rows = lax.broadcasted_iota(jnp.int32, (CHUNK, LANES), 0)
    cols = lax.broadcasted_iota(jnp.int32, (CHUNK, LANES), 1) % HEAD_SIZE
    strict = cols < rows
    incl = cols <= rows
    incl2 = jnp.concatenate([incl, incl], axis=1)
    cr = lax.broadcasted_iota(jnp.int32, (CHUNK, CHUNK), 0)
    cc = lax.broadcasted_iota(jnp.int32, (CHUNK, CHUNK), 1)
    cum_l = _bf((cc <= cr).astype(F32))

    def seg_sum(z):
        s0 = jnp.sum(jnp.where(head0, z, 0.0), axis=-1, keepdims=True)
        s1 = jnp.sum(jnp.where(head0, 0.0, z), axis=-1, keepdims=True)
        return jnp.where(head0, s0, s1)

    def stack(z):
        return _stack(z, head0)

    def pair_transpose(z):
        zero = jnp.zeros_like(z)
        bd = jnp.concatenate([jnp.where(head0, z, zero), jnp.where(head0, zero, z)], axis=0)
        bd_t = bd.T
        return bd_t[:CHUNK] + bd_t[CHUNK:]

    lo = lora_ref[...]
    zw = w0_ref[...] + _mm(_bf(jnp.tanh(lo)), wup_ref[...])
    nz = -zw
    softplus = jnp.maximum(nz, 0.0) + jnp.log(1.0 + jnp.exp(-jnp.abs(nz)))
    w_log = -softplus - 0.5
    lw_all = -jnp.exp(w_log)
    alr_all = jax.nn.sigmoid(a0_ref[...] + _mm(_bf(lo), aup_ref[...]))

    pairs = range(n_pairs)
    lanes_of = [slice(p * LANES, (p + 1) * LANES) for p in pairs]
    r, v, lw, kh, bb, aa, bonus = [], [], [], [], [], [], []
    for ls in lanes_of:
        r_p = r_ref[:, ls]
        k_p = k_ref[:, ls]
        v_p = v_ref[:, ls]
        alr = alr_all[:, ls]
        kkr = k_p * kk_ref[:, ls]
        kk = kkr / jnp.maximum(jnp.sqrt(seg_sum(kkr * kkr)), 1e-12)
        kh_p = k_p * (1.0 + (alr - 1.0) * ka_ref[:, ls])
        r.append(r_p)
        v.append(v_p)
        lw.append(lw_all[:, ls])
        kh.append(kh_p)
        bb.append(kk * alr)
        aa.append(-kk)
        bonus.append(seg_sum(r_p * kh_p * rk_ref[:, ls]) * v_p)

    rsl = lambda c: slice(c * CHUNK, (c + 1) * CHUNK)

    def precompute(units):
        cut = lambda arr: [arr[p][rsl(c)] for p, c in units]
        r_u, v_u, lw_u, kh_u, bb_u, aa_u = cut(r), cut(v), cut(lw), cut(kh), cut(bb), cut(aa)
        g = [_cumsum_rows(cum_l, z) for z in lw_u]
        g_end = [z[CHUNK - 1:CHUNK, :] for z in g]
        e_pos = [jnp.exp(z) for z in g]
        e_neg = [jnp.exp(-z) for z in g]
        e_tail = [jnp.exp(ze - z) for z, ze in zip(g, g_end)]
        r_t = [_bf(x * e) for x, e in zip(r_u, e_pos)]
        a_t = [_bf(x * jnp.exp(z - l)) for x, z, l in zip(aa_u, g, lw_u)]
        b_ts = [stack(x * e) for x, e in zip(bb_u, e_neg)]
        k_ts = [stack(x * e) for x, e in zip(kh_u, e_neg)]
        b_hs = [stack(x * e) for x, e in zip(bb_u, e_tail)]
        k_hs = [stack(x * e) for x, e in zip(kh_u, e_tail)]
        v_s = [stack(x) for x in v_u]
        v_t = [_bf(pair_transpose(x)) for x in v_u]
        decay = [jnp.exp(z) for z in g_end]

        gram = [_mm_nt(jnp.concatenate([a, rr], axis=0), jnp.concatenate([b, kx], axis=0))
                for a, rr, b, kx in zip(a_t, r_t, b_ts, k_ts)]
        a_ab = [jnp.where(strict, z[:CHUNK, :LANES], 0.0) for z in gram]
        a_ak = [_bf(jnp.where(strict, z[:CHUNK, LANES:], 0.0)) for z in gram]
        a_r = [_bf(jnp.where(incl2, z[CHUNK:], 0.0)) for z in gram]

        t_inv = [_bf(z) for z in _unit_lower_inverse(a_ab, rows, cols, head0)]
        akv = [_mm(a, vs) for a, vs in zip(a_ak, v_s)]
        w12 = [_mm(t, jnp.concatenate([stack(a), stack(z)], axis=1))
               for t, a, z in zip(t_inv, a_t, akv)]
        w1 = [z[:, :LANES] for z in w12]
        w2 = [z[:, LANES:] for z in w12]
        w1_b = [_bf(z) for z in w1]
        w1_t = [_bf(pair_transpose(z)) for z in w1]
        w2_t = [_bf(pair_transpose(z)) for z in w2]
        m_cs = [stack(_mm(wt, b)) for wt, b in zip(w1_t, b_hs)]
        n_c = [_mm(jnp.concatenate([wt, vx], axis=1), jnp.concatenate([b, kx], axis=0))
               for wt, vx, b, kx in zip(w2_t, v_t, b_hs, k_hs)]
        return dict(zip(units, zip(r_t, v_s, decay, a_r, w1_b, w2, m_cs, n_c)))

    pre = precompute([(p, c) for c in range(n_chunks) for p in pairs])

    s = [s_ref[p] for p in pairs]
    us_prev = [None] * n_pairs
    for c in range(n_chunks + 1):
        us_now = [None] * n_pairs
        if c < n_chunks:
            for p in pairs:
                r_t, _, decay, _, w1_b, _, m_cs, n_c = pre[(p, c)]
                s_b = _bf(s[p])
                s_s = stack(s[p])
                s[p] = s[p] * decay + _mm(s_b, m_cs) + n_c
                us_now[p] = _mm_nt(jnp.concatenate([w1_b, r_t], axis=0), s_s)
        if c > 0:
            for p in pairs:
                _, v_s, _, a_r, _, w2, _, _ = pre[(p, c - 1)]
                u = us_prev[p][:CHUNK] + w2
                y_ref[rsl(c - 1), lanes_of[p]] = us_prev[p][CHUNK:] + _mm(
                    a_r, jnp.concatenate([stack(u), v_s], axis=0))
        us_prev = us_now
    for p in pairs:
        s_ref[p] = s[p]

    inv_n = 1.0 / HEAD_SIZE
    for p, ls in zip(pairs, lanes_of):
        y = y_ref[:, ls]
        mu = seg_sum(y) * inv_n
        d = y - mu
        var = seg_sum(d * d) * inv_n
        yn = d * lax.rsqrt(var + GN_EPS) * lng_ref[:, ls] + lnb_ref[:, ls]
        o_ref[:, ls] = ((yn + bonus[p]) * gate_ref[:, ls]).astype(o_ref.dtype)


def _rwkv(p3, gate3, w0, a0, k_k, k_a, r_k, lnx_g, lnx_b, wup, aup, rwkv_w, tt=128):
    b, t, _ = p3.shape
    width = PAIRS_PER_STEP * LANES
    n_steps = rwkv_w // width
    lora_blk = (3 * rwkv_w) // LORA_PAD
    vec = lambda z: z.reshape(1, rwkv_w).astype(F32)
    col = lambda off: pl.BlockSpec((None, tt, width), lambda bi, hp, ti: (bi, ti, off + hp))
    par = pl.BlockSpec((1, width), lambda bi, hp, ti: (0, hp))
    up = pl.BlockSpec((LORA_PAD, width), lambda bi, hp, ti: (0, hp))
    kern = functools.partial(_rwkv_kernel, n_chunks=tt // CHUNK, n_pairs=PAIRS_PER_STEP)
    return pl.pallas_call(
        kern,
        out_shape=jax.ShapeDtypeStruct((b, t, rwkv_w), BF16),
        grid=(b, n_steps, t // tt),
        in_specs=[col(0), col(n_steps), col(2 * n_steps),
                  pl.BlockSpec((None, tt, LORA_PAD), lambda bi, hp, ti: (bi, ti, lora_blk)),
                  col(0),
                  par, par, par, par, par, par, par, up, up],
        out_specs=col(0),
        scratch_shapes=[pltpu.VMEM((PAIRS_PER_STEP, CHUNK, LANES), F32),
                        pltpu.VMEM((tt, width), F32)],
        compiler_params=_cparams(("parallel", "parallel", "arbitrary")),
        name="rwkv7",
    )(p3, p3, p3, p3, gate3, vec(w0), vec(a0), vec(k_k), vec(k_a), vec(r_k), vec(lnx_g),
      vec(lnx_b), wup, aup)


HALO = 32
CONV_ROWS = 64
CONV_LANES = 128


def _conv_kernel(u_ref, gate_ref, cw_ref, cb_ref, lg_ref, lb_ref, w_ref, b_ref, o_ref, buf_ref,
                 acc_ref, *, tt):
    cw_all = buf_ref.shape[1]

    @pl.when(pl.program_id(1) == 0)
    def _():
        buf_ref[0:HALO, :] = jnp.zeros((HALO, cw_all), F32)

    buf_ref[HALO:HALO + tt, :] = u_ref[...]
    base = HALO - (CONV_K - 1)

    win_rows = CONV_ROWS + HALO

    def lane_block(lb, carry):
        ls = pl.ds(pl.multiple_of(lb * CONV_LANES, CONV_LANES), CONV_LANES)
        bias = cb_ref[:, ls]
        for rb in range(tt // CONV_ROWS):
            win = buf_ref[rb * CONV_ROWS:rb * CONV_ROWS + win_rows, ls]
            acc = jnp.zeros((CONV_ROWS, CONV_LANES), F32) + bias
            for res in range(SUBLANES):
                sub = (base + res) % SUBLANES
                rot = win if sub == 0 else pltpu.roll(win, win_rows - sub, axis=0)
                for j in range(res, CONV_K, SUBLANES):
                    lo_row = base + j - sub
                    acc = acc + rot[lo_row:lo_row + CONV_ROWS] * cw_ref[pl.ds(j, 1), ls]
            acc_ref[rb * CONV_ROWS:(rb + 1) * CONV_ROWS, ls] = acc
        return carry

    lax.fori_loop(0, cw_all // CONV_LANES, lane_block, 0)
    tail = buf_ref[tt:tt + HALO, :]
    buf_ref[0:HALO, :] = tail

    acc = acc_ref[...]
    mu = jnp.mean(acc, axis=-1, keepdims=True)
    d = acc - mu
    var = jnp.mean(d * d, axis=-1, keepdims=True)
    c = d * lax.rsqrt(var + LN_EPS) * lg_ref[...] + lb_ref[...]
    c = c * jax.nn.sigmoid(c)
    z = jnp.dot(c.astype(BF16), w_ref[...], preferred_element_type=F32) + b_ref[...]
    o_ref[...] = (z * gate_ref[...]).astype(o_ref.dtype)


def _conv(u3, gate3, conv_w, conv_b, cln_g, cln_b, w_pw2, b_pw2, tt=256):
    b, t, cw = u3.shape
    vec = lambda z: z.reshape(1, cw).astype(F32)
    par = pl.BlockSpec((1, cw), lambda bi, ti: (0, 0))
    row = pl.BlockSpec((None, tt, cw), lambda bi, ti: (bi, ti, 0))
    kern = functools.partial(_conv_kernel, tt=tt)
    return pl.pallas_call(
        kern,
        out_shape=jax.ShapeDtypeStruct((b, t, cw), BF16),
        grid=(b, t // tt),
        in_specs=[row, row,
                  pl.BlockSpec((CONV_K, cw), lambda bi, ti: (0, 0)),
                  par, par, par,
                  pl.BlockSpec((cw, cw), lambda bi, ti: (0, 0)),
                  par],
        out_specs=row,
        scratch_shapes=[pltpu.VMEM((HALO + tt, cw), F32), pltpu.VMEM((tt, cw), F32)],
        compiler_params=_cparams(("parallel", "arbitrary")),
        name="conv_group",
    )(u3, gate3, conv_w.astype(F32), vec(conv_b), vec(cln_g), vec(cln_b), w_pw2.astype(BF16),
      vec(b_pw2))


def _outproj_kernel(ya_ref, yb_ref, wa_ref, wb_ref, x_ref, g_ref, o_ref, *, tn, n_steps):
    j = pl.program_id(1)
    acc = jnp.dot(ya_ref[...], wa_ref[...], preferred_element_type=F32)
    acc = acc + jnp.dot(yb_ref[...], wb_ref[...], preferred_element_type=F32)
    o_ref[:, pl.ds(pl.multiple_of(j * tn, tn), tn)] = acc

    @pl.when(j == n_steps - 1)
    def _():
        out = o_ref[...]
        ms = jnp.mean(out * out, axis=-1, keepdims=True)
        o_ref[...] = x_ref[...] + out * lax.rsqrt(ms + NORM_EPS) * g_ref[...]


def _outproj(ya, yb, w, x2, g, tm=512, tn=512):
    m, ka = ya.shape
    kb = yb.shape[1]
    d = w.shape[1]
    kern = functools.partial(_outproj_kernel, tn=tn, n_steps=d // tn)
    return pl.pallas_call(
        kern,
        out_shape=jax.ShapeDtypeStruct((m, d), F32),
        grid=(m // tm, d // tn),
        in_specs=[pl.BlockSpec((tm, ka), lambda i, j: (i, 0)),
                  pl.BlockSpec((tm, kb), lambda i, j: (i, 0)),
                  pl.BlockSpec((ka, tn), lambda i, j: (0, j)),
                  pl.BlockSpec((kb, tn), lambda i, j: (ka // kb, j)),
                  pl.BlockSpec((tm, d), lambda i, j: (i, 0)),
                  pl.BlockSpec((1, d), lambda i, j: (0, 0))],
        out_specs=pl.BlockSpec((tm, d), lambda i, j: (i, 0)),
        compiler_params=_cparams(("parallel", "arbitrary")),
        name="outproj_norm",
    )(ya, yb, w, w, x2, g.reshape(1, d).astype(F32))


def kernel(x, norm_pre_g, w_in, mu_shift, w0, w_lora_up, a0, a_lora_up, k_k, k_a, r_k, lnx_g,
           lnx_b, conv_w, conv_b, cln_g, cln_b, w_pw2, b_pw2, w_out, norm_post_g):
    bsz, seq, d_model = x.shape
    rwkv_w = w0.shape[0]
    conv_wd = conv_b.shape[0]
    assert rwkv_w == conv_wd
    shift_cols = 3 * rwkv_w + LORA_W + LORA_A
    shift_pad = 3 * rwkv_w + LORA_PAD
    m = bsz * seq
    x2 = x.reshape(m, d_model)

    pad = shift_pad - shift_cols
    w_shift = w_in[:, :shift_pad].astype(BF16)
    w_rest = w_in[:, shift_cols:].astype(BF16)
    mu_p = jnp.concatenate([mu_shift, jnp.zeros((pad,), mu_shift.dtype)]).reshape(1, -1).astype(F32)
    wup = jnp.zeros((LORA_PAD, rwkv_w), BF16).at[:LORA_W].set(w_lora_up.astype(BF16))
    aup = jnp.zeros((LORA_PAD, rwkv_w), BF16).at[LORA_W:LORA_W + LORA_A].set(a_lora_up.astype(BF16))

    h = _prenorm(x2, norm_pre_g.astype(F32))
    p = _mm_shift(h, w_shift, mu_p, seq)
    gate_r, gate_c, u = _mm_gates(h, w_rest, rwkv_w)

    to3 = lambda z: z.reshape(bsz, seq, -1)
    y_rwkv = _rwkv(to3(p), to3(gate_r), w0, a0, k_k, k_a, r_k, lnx_g, lnx_b, wup, aup, rwkv_w)
    y_conv = _conv(to3(u), to3(gate_c), conv_w, conv_b, cln_g, cln_b, w_pw2, b_pw2)

    out = _outproj(y_rwkv.reshape(m, rwkv_w), y_conv.reshape(m, conv_wd), w_out.astype(BF16), x2,
                   norm_post_g)
    return out.reshape(bsz, seq, d_model)
```

```python
import functools

import jax
import jax.numpy as jnp
from jax import lax
from jax.experimental import pallas as pl
from jax.experimental.pallas import tpu as pltpu

F32 = jnp.float32
BF16 = jnp.bfloat16

HEAD_SIZE = 64
LORA_W = 96
LORA_A = 96
CONV_K = 31
NORM_EPS = 1e-6
LN_EPS = 1e-5
GN_EPS = 1e-5 * HEAD_SIZE

LANES = 128
SUBLANES = 8
LORA_PAD = 256
CHUNK = 64
PAIRS_PER_STEP = 16
VMEM_LIMIT = 56 * 1024 * 1024

assert CHUNK == HEAD_SIZE


def _cparams(sem):
    return pltpu.CompilerParams(dimension_semantics=sem, vmem_limit_bytes=VMEM_LIMIT)


def _prenorm_kernel(x_ref, g_ref, o_ref):
    x = x_ref[...]
    ms = jnp.mean(x * x, axis=-1, keepdims=True)
    o_ref[...] = (x * lax.rsqrt(ms + NORM_EPS) * g_ref[...]).astype(o_ref.dtype)


def _prenorm(x2, g, tm=256):
    m, d = x2.shape
    return pl.pallas_call(
        _prenorm_kernel,
        out_shape=jax.ShapeDtypeStruct((m, d), BF16),
        grid=(m // tm,),
        in_specs=[pl.BlockSpec((tm, d), lambda i: (i, 0)),
                  pl.BlockSpec((1, d), lambda i: (0, 0))],
        out_specs=pl.BlockSpec((tm, d), lambda i: (i, 0)),
        compiler_params=_cparams(("parallel",)),
        name="prenorm",
    )(x2, g.reshape(1, d))


def _wcast_kernel(w_ref, ws_ref, wr_ref, *, n_shift, rest0):
    w = w_ref[...]
    ws_ref[...] = w[:, :n_shift].astype(ws_ref.dtype)
    wr_ref[...] = w[:, rest0:].astype(wr_ref.dtype)


def _wcast(w, n_shift, rest0, tk=128):
    k, n = w.shape
    kern = functools.partial(_wcast_kernel, n_shift=n_shift, rest0=rest0)
    return pl.pallas_call(
        kern,
        out_shape=(jax.ShapeDtypeStruct((k, n_shift), BF16),
                   jax.ShapeDtypeStruct((k, n - rest0), BF16)),
        grid=(k // tk,),
        in_specs=[pl.BlockSpec((tk, n), lambda i: (i, 0))],
        out_specs=(pl.BlockSpec((tk, n_shift), lambda i: (i, 0)),
                   pl.BlockSpec((tk, n - rest0), lambda i: (i, 0))),
        compiler_params=_cparams(("parallel",)),
        name="weight_cast",
    )(w)


def _mm_shift_kernel(h_ref, w_ref, mu_ref, o_ref, carry_ref, *, tiles_per_seq):
    i = pl.program_id(1)
    p = jnp.dot(h_ref[...], w_ref[...], preferred_element_type=F32)

    @pl.when(i % tiles_per_seq == 0)
    def _():
        carry_ref[...] = jnp.zeros_like(carry_ref)

    rows = lax.broadcasted_iota(jnp.int32, p.shape, 0)
    prev = jnp.where(rows == 0, carry_ref[...], pltpu.roll(p, 1, axis=0))
    carry_ref[...] = p[-1:, :]
    o_ref[...] = p + (prev - p) * mu_ref[...]


def _mm_shift(h, w, mu, seq_len, tm=512, tn=1280):
    m, k = h.shape
    n = w.shape[1]
    tm = min(tm, seq_len)
    kern = functools.partial(_mm_shift_kernel, tiles_per_seq=seq_len // tm)
    return pl.pallas_call(
        kern,
        out_shape=jax.ShapeDtypeStruct((m, n), F32),
        grid=(n // tn, m // tm),
        in_specs=[pl.BlockSpec((tm, k), lambda j, i: (i, 0)),
                  pl.BlockSpec((k, tn), lambda j, i: (0, j)),
                  pl.BlockSpec((1, tn), lambda j, i: (0, j))],
        out_specs=pl.BlockSpec((tm, tn), lambda j, i: (i, j)),
        scratch_shapes=[pltpu.VMEM((1, tn), F32)],
        compiler_params=_cparams(("parallel", "arbitrary")),
        name="inproj_shift",
    )(h, w, mu)


def _mm_gates_kernel(h_ref, wgr_ref, wgv_ref, wgg_ref, wgc_ref, gr_ref, gc_ref, u_ref):
    h = h_ref[...]
    dot = lambda w_ref: jnp.dot(h, w_ref[...], preferred_element_type=F32)
    pr = dot(wgr_ref)
    gr_ref[...] = pr * jax.nn.sigmoid(pr)
    pc = dot(wgc_ref)
    gc_ref[...] = pc * jax.nn.sigmoid(pc)
    u_ref[...] = dot(wgv_ref) * jax.nn.sigmoid(dot(wgg_ref))


def _mm_gates(h, w, width, tm=1024, tn=256):
    m, k = h.shape
    tm = min(tm, m)
    per = width // tn
    wspec = lambda g: pl.BlockSpec((k, tn), lambda i, j: (0, g * per + j))
    ospec = pl.BlockSpec((tm, tn), lambda i, j: (i, j))
    oshape = jax.ShapeDtypeStruct((m, width), F32)
    return pl.pallas_call(
        _mm_gates_kernel,
        out_shape=(oshape, oshape, oshape),
        grid=(m // tm, per),
        in_specs=[pl.BlockSpec((tm, k), lambda i, j: (i, 0)), wspec(0), wspec(1), wspec(2), wspec(3)],
        out_specs=(ospec, ospec, ospec),
        compiler_params=_cparams(("parallel", "arbitrary")),
        name="inproj_gates",
    )(h, w, w, w, w)


def _bf(z):
    return z.astype(BF16)


def _mm(a, b):
    return jnp.dot(a, b, preferred_element_type=F32)


def _mm_nt(a, b):
    return lax.dot_general(a, b, (((1,), (1,)), ((), ())), preferred_element_type=F32)


def _cumsum_rows(cum_l, z):
    hi = _bf(z)
    r1 = z - hi.astype(F32)
    mid = _bf(r1)
    lo = _bf(r1 - mid.astype(F32))
    two = _mm(cum_l, jnp.concatenate([hi, mid], axis=1))
    return two[:, :LANES] + two[:, LANES:] + _mm(cum_l, lo)


def _stack(z, head0):
    zb = _bf(z)
    zero = jnp.zeros_like(zb)
    return jnp.concatenate([jnp.where(head0, zb, zero), jnp.where(head0, zero, zb)], axis=0)


def _unit_lower_inverse(a_list, rows, cols, head0):
    eye = (rows == cols).astype(F32)
    same8 = (rows // 8) == (cols // 8)
    a8 = [jnp.where(same8, a, 0.0) for a in a_list]
    a8_s = [_stack(z, head0) for z in a8]
    a8_2 = [_mm(_bf(z), zs) for z, zs in zip(a8, a8_s)]
    yield
    a8_2s = [_stack(z, head0) for z in a8_2]
    a8_4s = [_stack(_mm(_bf(z), zs), head0) for z, zs in zip(a8_2, a8_2s)]
    yield
    p1 = [eye + z for z in a8]
    p1 = [p + _mm(_bf(p), zs) for p, zs in zip(p1, a8_2s)]
    yield
    t = [p + _mm(_bf(p), zs) for p, zs in zip(p1, a8_4s)]
    yield
    blk = 8
    while blk < CHUNK:
        inner = (rows // blk) == (cols // blk)
        outer = (rows // (2 * blk)) == (cols // (2 * blk))
        join = outer & jnp.logical_not(inner)
        e_s = [_stack(jnp.where(join, a, 0.0), head0) for a in a_list]
        t_b = [_bf(z) for z in t]
        t_s = [_stack(z, head0) for z in t]
        te = [_bf(_mm(tb, es)) for tb, es in zip(t_b, e_s)]
        yield
        t = [z + _mm(tez, ts) for z, tez, ts in zip(t, te, t_s)]
        yield
        blk *= 2
    return t


def _interleave(*gens):
    live = list(gens)
    while live:
        for gen in list(live):
            try:
                next(gen)
            except StopIteration:
                live.remove(gen)


def _rwkv_kernel(r_ref, k_ref, v_ref, lora_ref, gate_ref, w0_ref, a0_ref, kk_ref, ka_ref,
                 rk_ref, lng_ref, lnb_ref, wup_ref, aup_ref, o_ref, s_ref, y_ref, *, n_chunks,
                 n_pairs):
    @pl.when(pl.program_id(2) == 0)
    def _():
        s_ref[...] = jnp.zeros_like(s_ref)

    lane = lax.broadcasted_iota(jnp.int32, (1, LANES), 1)
    head0 = lane < HEAD_SIZE
    rows = lax.broadcasted_iota(jnp.int32, (CHUNK, LANES), 0)
    cols = lax.broadcasted_iota(jnp.int32, (CHUNK, LANES), 1) % HEAD_SIZE
    strict = cols < rows
    incl = cols <= rows
    incl2 = jnp.concatenate([incl, incl], axis=1)
    cr = lax.broadcasted_iota(jnp.int32, (CHUNK, CHUNK), 0)
    cc = lax.broadcasted_iota(jnp.int32, (CHUNK, CHUNK), 1)
    cum_l = _bf((cc <= cr).astype(F32))

    def seg_sum(z):
        s0 = jnp.sum(jnp.where(head0, z, 0.0), axis=-1, keepdims=True)
        s1 = jnp.sum(jnp.where(head0, 0.0, z), axis=-1, keepdims=True)
        return jnp.where(head0, s0, s1)

    def stack(z):
        return _stack(z, head0)

    def pair_transpose(z):
        zero = jnp.zeros_like(z)
        bd = jnp.concatenate([jnp.where(head0, z, zero), jnp.where(head0, zero, z)], axis=0)
        bd_t = bd.T
        return bd_t[:CHUNK] + bd_t[CHUNK:]

    lo = lora_ref[...]
    zw = w0_ref[...] + _mm(_bf(jnp.tanh(lo)), wup_ref[...])
    nz = -zw
    softplus = jnp.maximum(nz, 0.0) + jnp.log(1.0 + jnp.exp(-jnp.abs(nz)))
    w_log = -softplus - 0.5
    lw_all = -jnp.exp(w_log)
    alr_all = jax.nn.sigmoid(a0_ref[...] + _mm(_bf(lo), aup_ref[...]))

    lanes_of = [slice(p * LANES, (p + 1) * LANES) for p in range(n_pairs)]
    rsl = lambda c: slice(c * CHUNK, (c + 1) * CHUNK)
    waves = [list(range(0, n_pairs // 2)), list(range(n_pairs // 2, n_pairs))]
    units_of = [[(p, c) for c in range(n_chunks) for p in wave] for wave in waves]
    g_of = [[_cumsum_rows(cum_l, lw_all[rsl(c), lanes_of[p]]) for p, c in units]
            for units in units_of]
    pre = {}
    bonus = {}

    def prepare(wave):
        r, v, lw, kh, bb, aa = {}, {}, {}, {}, {}, {}
        for p in waves[wave]:
            ls = lanes_of[p]
            r_p = r_ref[:, ls]
            k_p = k_ref[:, ls]
            v_p = v_ref[:, ls]
            alr = alr_all[:, ls]
            kkr = k_p * kk_ref[:, ls]
            kk = kkr * lax.rsqrt(jnp.maximum(seg_sum(kkr * kkr), 1e-24))
            kh_p = k_p * (1.0 + (alr - 1.0) * ka_ref[:, ls])
            r[p], v[p], lw[p], kh[p], bb[p], aa[p] = r_p, v_p, lw_all[:, ls], kh_p, kk * alr, -kk
            bonus[p] = seg_sum(r_p * kh_p * rk_ref[:, ls]) * v_p
            yield
        for (p, c), g in zip(units_of[wave], g_of[wave]):
            lw_u = lw[p][rsl(c)]
            g_end = g[CHUNK - 1:CHUNK, :]
            e_pos = jnp.exp(g)
            e_neg = jnp.exp(-g)
            e_tail = jnp.exp(g_end - g)
            bb_u, kh_u, v_u = bb[p][rsl(c)], kh[p][rsl(c)], v[p][rsl(c)]
            pre[(p, c)] = dict(
                r_t=_bf(r[p][rsl(c)] * e_pos),
                a_t=_bf(aa[p][rsl(c)] * jnp.exp(g - lw_u)),
                b_ts=stack(bb_u * e_neg), k_ts=stack(kh_u * e_neg),
                b_hs=stack(bb_u * e_tail), k_hs=stack(kh_u * e_tail),
                v_s=stack(v_u), v_t=_bf(pair_transpose(v_u)), decay=jnp.exp(g_end))
            yield

    def precompute(wave):
        units = units_of[wave]
        get = lambda name: [pre[unit][name] for unit in units]
        r_t, a_t, b_ts, k_ts, b_hs, k_hs, v_s, v_t = (
            get(name) for name in ("r_t", "a_t", "b_ts", "k_ts", "b_hs", "k_hs", "v_s", "v_t"))
        gram = [_mm_nt(jnp.concatenate([a, rr], axis=0), jnp.concatenate([b, kx], axis=0))
                for a, rr, b, kx in zip(a_t, r_t, b_ts, k_ts)]
        yield
        a_ab = [jnp.where(strict, z[:CHUNK, :LANES], 0.0) for z in gram]
        a_ak = [_bf(jnp.where(strict, z[:CHUNK, LANES:], 0.0)) for z in gram]
        a_r = [_bf(jnp.where(incl2, z[CHUNK:], 0.0)) for z in gram]
        akv = [_mm(a, vs) for a, vs in zip(a_ak, v_s)]
        yield
        t_inv = yield from _unit_lower_inverse(a_ab, rows, cols, head0)
        w12 = [_mm(_bf(t), jnp.concatenate([stack(a), stack(z)], axis=1))
               for t, a, z in zip(t_inv, a_t, akv)]
        yield
        w1 = [z[:, :LANES] for z in w12]
        w2 = [z[:, LANES:] for z in w12]
        w1_b = [_bf(z) for z in w1]
        w1_t = [_bf(pair_transpose(z)) for z in w1]
        w2_t = [_bf(pair_transpose(z)) for z in w2]
        yield
        m_cs = [stack(_mm(wt, b)) for wt, b in zip(w1_t, b_hs)]
        yield
        n_c = [_mm(jnp.concatenate([wt, vx], axis=1), jnp.concatenate([b, kx], axis=0))
               for wt, vx, b, kx in zip(w2_t, v_t, b_hs, k_hs)]
        for unit, ar, wb, w2u, mc, nc in zip(units, a_r, w1_b, w2, m_cs, n_c):
            pre[unit].update(a_r=ar, w1_b=wb, w2=w2u, m_cs=mc, n_c=nc)
        yield

    def chain(wave):
        s = {p: s_ref[p] for p in waves[wave]}
        us_prev = {}
        for c in range(n_chunks + 1):
            us_now = {}
            if c < n_chunks:
                for p in waves[wave]:
                    q = pre[(p, c)]
                    s_b = _bf(s[p])
                    s_s = stack(s[p])
                    s[p] = s[p] * q["decay"] + _mm(s_b, q["m_cs"]) + q["n_c"]
                    us_now[p] = _mm_nt(jnp.concatenate([q["w1_b"], q["r_t"]], axis=0), s_s)
            if c > 0:
                for p in waves[wave]:
                    q = pre[(p, c - 1)]
                    u = us_prev[p][:CHUNK] + q["w2"]
                    y_ref[rsl(c - 1), lanes_of[p]] = us_prev[p][CHUNK:] + _mm(
                        q["a_r"], jnp.concatenate([stack(u), q["v_s"]], axis=0))
            us_prev = us_now
            yield
        for p in waves[wave]:
            s_ref[p] = s[p]

    _interleave(prepare(0))
    _interleave(precompute(0), prepare(1))
    _interleave(precompute(1), chain(0))
    _interleave(chain(1))

    inv_n = 1.0 / HEAD_SIZE
    for p, ls in enumerate(lanes_of):
        y = y_ref[:, ls]
        mu = seg_sum(y) * inv_n
        d = y - mu
        var = seg_sum(d * d) * inv_n
        yn = d * lax.rsqrt(var + GN_EPS) * lng_ref[:, ls] + lnb_ref[:, ls]
        o_ref[:, ls] = ((yn + bonus[p]) * gate_ref[:, ls]).astype(o_ref.dtype)


def _rwkv(p3, gate3, w0, a0, k_k, k_a, r_k, lnx_g, lnx_b, wup, aup, rwkv_w, tt=128):
    b, t, _ = p3.shape
    width = PAIRS_PER_STEP * LANES
    n_steps = rwkv_w // width
    lora_blk = (3 * rwkv_w) // LORA_PAD
    vec = lambda z: z.reshape(1, rwkv_w).astype(F32)
    col = lambda off: pl.BlockSpec((None, tt, width), lambda bi, hp, ti: (bi, ti, off + hp))
    par = pl.BlockSpec((1, width), lambda bi, hp, ti: (0, hp))
    up = pl.BlockSpec((LORA_PAD, width), lambda bi, hp, ti: (0, hp))
    kern = functools.partial(_rwkv_kernel, n_chunks=tt // CHUNK, n_pairs=PAIRS_PER_STEP)
    return pl.pallas_call(
        kern,
        out_shape=jax.ShapeDtypeStruct((b, t, rwkv_w), BF16),
        grid=(b, n_steps, t // tt),
        in_specs=[col(0), col(n_steps), col(2 * n_steps),
                  pl.BlockSpec((None, tt, LORA_PAD), lambda bi, hp, ti: (bi, ti, lora_blk)),
                  col(0),
                  par, par, par, par, par, par, par, up, up],
        out_specs=col(0),
        scratch_shapes=[pltpu.VMEM((PAIRS_PER_STEP, CHUNK, LANES), F32),
                        pltpu.VMEM((tt, width), F32)],
        compiler_params=_cparams(("parallel", "parallel", "arbitrary")),
        name="rwkv7",
    )(p3, p3, p3, p3, gate3, vec(w0), vec(a0), vec(k_k), vec(k_a), vec(r_k), vec(lnx_g),
      vec(lnx_b), wup, aup)


HALO = 32
CONV_ROWS = 64
CONV_LANES = 128


def _conv_kernel(u_ref, gate_ref, cw_ref, cb_ref, lg_ref, lb_ref, w_ref, b_ref, o_ref, buf_ref,
                 acc_ref, *, tt):
    cw_all = buf_ref.shape[1]

    @pl.when(pl.program_id(1) == 0)
    def _():
        buf_ref[0:HALO, :] = jnp.zeros((HALO, cw_all), F32)

    buf_ref[HALO:HALO + tt, :] = u_ref[...]
    base = HALO - (CONV_K - 1)

    win_rows = CONV_ROWS + HALO

    def lane_block(lb, carry):
        ls = pl.ds(pl.multiple_of(lb * CONV_LANES, CONV_LANES), CONV_LANES)
        bias = cb_ref[:, ls]
        for rb in range(tt // CONV_ROWS):
            win = buf_ref[rb * CONV_ROWS:rb * CONV_ROWS + win_rows, ls]
            acc = jnp.zeros((CONV_ROWS, CONV_LANES), F32) + bias
            for res in range(SUBLANES):
                sub = (base + res) % SUBLANES
                rot = win if sub == 0 else pltpu.roll(win, win_rows - sub, axis=0)
                for j in range(res, CONV_K, SUBLANES):
                    lo_row = base + j - sub
                    acc = acc + rot[lo_row:lo_row + CONV_ROWS] * cw_ref[pl.ds(j, 1), ls]
            acc_ref[rb * CONV_ROWS:(rb + 1) * CONV_ROWS, ls] = acc
        return carry

    lax.fori_loop(0, cw_all // CONV_LANES, lane_block, 0)
    tail = buf_ref[tt:tt + HALO, :]
    buf_ref[0:HALO, :] = tail

    acc = acc_ref[...]
    mu = jnp.mean(acc, axis=-1, keepdims=True)
    d = acc - mu
    var = jnp.mean(d * d, axis=-1, keepdims=True)
    c = d * lax.rsqrt(var + LN_EPS) * lg_ref[...] + lb_ref[...]
    c = c * jax.nn.sigmoid(c)
    z = jnp.dot(c.astype(BF16), w_ref[...], preferred_element_type=F32) + b_ref[...]
    o_ref[...] = (z * gate_ref[...]).astype(o_ref.dtype)


def _conv(u3, gate3, conv_w, conv_b, cln_g, cln_b, w_pw2, b_pw2, tt=256):
    b, t, cw = u3.shape
    vec = lambda z: z.reshape(1, cw).astype(F32)
    par = pl.BlockSpec((1, cw), lambda bi, ti: (0, 0))
    row = pl.BlockSpec((None, tt, cw), lambda bi, ti: (bi, ti, 0))
    kern = functools.partial(_conv_kernel, tt=tt)
    return pl.pallas_call(
        kern,
        out_shape=jax.ShapeDtypeStruct((b, t, cw), BF16),
        grid=(b, t // tt),
        in_specs=[row, row,
                  pl.BlockSpec((CONV_K, cw), lambda bi, ti: (0, 0)),
                  par, par, par,
                  pl.BlockSpec((cw, cw), lambda bi, ti: (0, 0)),
                  par],
        out_specs=row,
        scratch_shapes=[pltpu.VMEM((HALO + tt, cw), F32), pltpu.VMEM((tt, cw), F32)],
        compiler_params=_cparams(("parallel", "arbitrary")),
        name="conv_group",
    )(u3, gate3, conv_w.astype(F32), vec(conv_b), vec(cln_g), vec(cln_b), w_pw2.astype(BF16),
      vec(b_pw2))


def _outproj_kernel(ya_ref, yb_ref, wa_ref, wb_ref, x_ref, g_ref, o_ref, *, tn, n_steps):
    j = pl.program_id(1)
    acc = jnp.dot(ya_ref[...], wa_ref[...], preferred_element_type=F32)
    acc = acc + jnp.dot(yb_ref[...], wb_ref[...], preferred_element_type=F32)
    o_ref[:, pl.ds(pl.multiple_of(j * tn, tn), tn)] = acc

    @pl.when(j == n_steps - 1)
    def _():
        out = o_ref[...]
        ms = jnp.mean(out * out, axis=-1, keepdims=True)
        o_ref[...] = x_ref[...] + out * lax.rsqrt(ms + NORM_EPS) * g_ref[...]


def _outproj(ya, yb, w, x2, g, tm=512, tn=512):
    m, ka = ya.shape
    kb = yb.shape[1]
    d = w.shape[1]
    kern = functools.partial(_outproj_kernel, tn=tn, n_steps=d // tn)
    return pl.pallas_call(
        kern,
        out_shape=jax.ShapeDtypeStruct((m, d), F32),
        grid=(m // tm, d // tn),
        in_specs=[pl.BlockSpec((tm, ka), lambda i, j: (i, 0)),
                  pl.BlockSpec((tm, kb), lambda i, j: (i, 0)),
                  pl.BlockSpec((ka, tn), lambda i, j: (0, j)),
                  pl.BlockSpec((kb, tn), lambda i, j: (ka // kb, j)),
                  pl.BlockSpec((tm, d), lambda i, j: (i, 0)),
                  pl.BlockSpec((1, d), lambda i, j: (0, 0))],
        out_specs=pl.BlockSpec((tm, d), lambda i, j: (i, 0)),
        compiler_params=_cparams(("parallel", "arbitrary")),
        name="outproj_norm",
    )(ya, yb, w, w, x2, g.reshape(1, d).astype(F32))


def kernel(x, norm_pre_g, w_in, mu_shift, w0, w_lora_up, a0, a_lora_up, k_k, k_a, r_k, lnx_g,
           lnx_b, conv_w, conv_b, cln_g, cln_b, w_pw2, b_pw2, w_out, norm_post_g):
    bsz, seq, d_model = x.shape
    rwkv_w = w0.shape[0]
    conv_wd = conv_b.shape[0]
    assert rwkv_w == conv_wd
    shift_cols = 3 * rwkv_w + LORA_W + LORA_A
    shift_pad = 3 * rwkv_w + LORA_PAD
    m = bsz * seq
    x2 = x.reshape(m, d_model)

    pad = shift_pad - shift_cols
    w_shift, w_rest = _wcast(w_in, shift_pad, shift_cols)
    mu_p = jnp.concatenate([mu_shift, jnp.zeros((pad,), mu_shift.dtype)]).reshape(1, -1).astype(F32)
    wup = jnp.zeros((LORA_PAD, rwkv_w), BF16).at[:LORA_W].set(w_lora_up.astype(BF16))
    aup = jnp.zeros((LORA_PAD, rwkv_w), BF16).at[LORA_W:LORA_W + LORA_A].set(a_lora_up.astype(BF16))

    h = _prenorm(x2, norm_pre_g.astype(F32))
    p = _mm_shift(h, w_shift, mu_p, seq)
    gate_r, gate_c, u = _mm_gates(h, w_rest, rwkv_w)

    to3 = lambda z: z.reshape(bsz, seq, -1)
    y_rwkv = _rwkv(to3(p), to3(gate_r), w0, a0, k_k, k_a, r_k, lnx_g, lnx_b, wup, aup, rwkv_w)
    y_conv = _conv(to3(u), to3(gate_c), conv_w, conv_b, cln_g, cln_b, w_pw2, b_pw2)

    out = _outproj(y_rwkv.reshape(m, rwkv_w), y_conv.reshape(m, conv_wd), w_out.astype(BF16), x2,
                   norm_post_g)
    return out.reshape(bsz, seq, d_model)
```

```python
import functools

import jax
import jax.numpy as jnp
from jax import lax
from jax.experimental import pallas as pl
from jax.experimental.pallas import tpu as pltpu

F32 = jnp.float32
BF16 = jnp.bfloat16

HEAD_SIZE = 64
LORA_W = 96
LORA_A = 96
CONV_K = 31
NORM_EPS = 1e-6
LN_EPS = 1e-5
GN_EPS = 1e-5 * HEAD_SIZE

LANES = 128
SUBLANES = 8
LORA_PAD = 256
CHUNK = 64
PAIRS_PER_STEP = 16
VMEM_LIMIT = 56 * 1024 * 1024

assert CHUNK == HEAD_SIZE


def _cparams(sem):
    return pltpu.CompilerParams(dimension_semantics=sem, vmem_limit_bytes=VMEM_LIMIT)


def _prenorm_kernel(x_ref, g_ref, o_ref):
    x = x_ref[...]
    ms = jnp.mean(x * x, axis=-1, keepdims=True)
    o_ref[...] = (x * lax.rsqrt(ms + NORM_EPS) * g_ref[...]).astype(o_ref.dtype)


def _prenorm(x2, g, tm=256):
    m, d = x2.shape
    return pl.pallas_call(
        _prenorm_kernel,
        out_shape=jax.ShapeDtypeStruct((m, d), BF16),
        grid=(m // tm,),
        in_specs=[pl.BlockSpec((tm, d), lambda i: (i, 0)),
                  pl.BlockSpec((1, d), lambda i: (0, 0))],
        out_specs=pl.BlockSpec((tm, d), lambda i: (i, 0)),
        compiler_params=_cparams(("parallel",)),
        name="prenorm",
    )(x2, g.reshape(1, d))


WCAST_PIECE = 64


def _wcast_kernel(a_ref, b_ref, o_ref):
    rows = jnp.concatenate([a_ref[...], b_ref[...]], axis=0)
    o_ref[...] = rows.T.astype(o_ref.dtype)


def _wcast(wt3, piece0, n_cols):
    k = wt3.shape[2]
    piece = lambda off: pl.BlockSpec((None, WCAST_PIECE, k), lambda i: (piece0 + 2 * i + off, 0, 0))
    return pl.pallas_call(
        _wcast_kernel,
        out_shape=jax.ShapeDtypeStruct((k, n_cols), BF16),
        grid=(n_cols // LANES,),
        in_specs=[piece(0), piece(1)],
        out_specs=pl.BlockSpec((k, LANES), lambda i: (0, i)),
        compiler_params=_cparams(("parallel",)),
        name="weight_cast",
    )(wt3, wt3)


def _mm_shift_kernel(h_ref, w_ref, mu_ref, o_ref, carry_ref, *, tiles_per_seq):
    i = pl.program_id(1)
    p = jnp.dot(h_ref[...], w_ref[...], preferred_element_type=F32)

    @pl.when(i % tiles_per_seq == 0)
    def _():
        carry_ref[...] = jnp.zeros_like(carry_ref)

    rows = lax.broadcasted_iota(jnp.int32, p.shape, 0)
    prev = jnp.where(rows == 0, carry_ref[...], pltpu.roll(p, 1, axis=0))
    carry_ref[...] = p[-1:, :]
    o_ref[...] = p + (prev - p) * mu_ref[...]


def _mm_shift(h, w, mu, seq_len, tm=512, tn=1280):
    m, k = h.shape
    n = w.shape[1]
    tm = min(tm, seq_len)
    kern = functools.partial(_mm_shift_kernel, tiles_per_seq=seq_len // tm)
    return pl.pallas_call(
        kern,
        out_shape=jax.ShapeDtypeStruct((m, n), F32),
        grid=(n // tn, m // tm),
        in_specs=[pl.BlockSpec((tm, k), lambda j, i: (i, 0)),
                  pl.BlockSpec((k, tn), lambda j, i: (0, j)),
                  pl.BlockSpec((1, tn), lambda j, i: (0, j))],
        out_specs=pl.BlockSpec((tm, tn), lambda j, i: (i, j)),
        scratch_shapes=[pltpu.VMEM((1, tn), F32)],
        compiler_params=_cparams(("parallel", "arbitrary")),
        name="inproj_shift",
    )(h, w, mu)


def _mm_gates_kernel(h_ref, wgr_ref, wgv_ref, wgg_ref, wgc_ref, gr_ref, gc_ref, u_ref):
    h = h_ref[...]
    dot = lambda w_ref: jnp.dot(h, w_ref[...], preferred_element_type=F32)
    pr = dot(wgr_ref)
    gr_ref[...] = pr * jax.nn.sigmoid(pr)
    pc = dot(wgc_ref)
    gc_ref[...] = pc * jax.nn.sigmoid(pc)
    u_ref[...] = dot(wgv_ref) * jax.nn.sigmoid(dot(wgg_ref))


def _mm_gates(h, w, width, tm=1024, tn=256):
    m, k = h.shape
    tm = min(tm, m)
    per = width // tn
    wspec = lambda g: pl.BlockSpec((k, tn), lambda i, j: (0, g * per + j))
    ospec = pl.BlockSpec((tm, tn), lambda i, j: (i, j))
    oshape = jax.ShapeDtypeStruct((m, width), F32)
    return pl.pallas_call(
        _mm_gates_kernel,
        out_shape=(oshape, oshape, oshape),
        grid=(m // tm, per),
        in_specs=[pl.BlockSpec((tm, k), lambda i, j: (i, 0)), wspec(0), wspec(1), wspec(2), wspec(3)],
        out_specs=(ospec, ospec, ospec),
        compiler_params=_cparams(("parallel", "arbitrary")),
        name="inproj_gates",
    )(h, w, w, w, w)


def _bf(z):
    return z.astype(BF16)


def _mm(a, b):
    return jnp.dot(a, b, preferred_element_type=F32)


def _mm_nt(a, b):
    return lax.dot_general(a, b, (((1,), (1,)), ((), ())), preferred_element_type=F32)


def _cumsum_rows(cum_l, z):
    hi = _bf(z)
    r1 = z - hi.astype(F32)
    mid = _bf(r1)
    lo = _bf(r1 - mid.astype(F32))
    two = _mm(cum_l, jnp.concatenate([hi, mid], axis=1))
    return two[:, :LANES] + two[:, LANES:] + _mm(cum_l, lo)


def _stack(z, head0):
    zb = _bf(z)
    zero = jnp.zeros_like(zb)
    return jnp.concatenate([jnp.where(head0, zb, zero), jnp.where(head0, zero, zb)], axis=0)


def _unit_lower_inverse(a_list, rows, cols, head0):
    eye = (rows == cols).astype(F32)
    same8 = (rows // 8) == (cols // 8)
    a8 = [jnp.where(same8, a, 0.0) for a in a_list]
    a8_s = [_stack(z, head0) for z in a8]
    a8_2 = [_mm(_bf(z), zs) for z, zs in zip(a8, a8_s)]
    yield
    a8_2s = [_stack(z, head0) for z in a8_2]
    a8_4s = [_stack(_mm(_bf(z), zs), head0) for z, zs in zip(a8_2, a8_2s)]
    yield
    p1 = [eye + z for z in a8]
    p1 = [p + _mm(_bf(p), zs) for p, zs in zip(p1, a8_2s)]
    yield
    t = [p + _mm(_bf(p), zs) for p, zs in zip(p1, a8_4s)]
    yield
    blk = 8
    while blk < CHUNK:
        inner = (rows // blk) == (cols // blk)
        outer = (rows // (2 * blk)) == (cols // (2 * blk))
        join = outer & jnp.logical_not(inner)
        e_s = [_stack(jnp.where(join, a, 0.0), head0) for a in a_list]
        t_b = [_bf(z) for z in t]
        t_s = [_stack(z, head0) for z in t]
        te = [_bf(_mm(tb, es)) for tb, es in zip(t_b, e_s)]
        yield
        t = [z + _mm(tez, ts) for z, tez, ts in zip(t, te, t_s)]
        yield
        blk *= 2
    return t


def _interleave(*gens):
    live = list(gens)
    while live:
        for gen in list(live):
            try:
                next(gen)
            except StopIteration:
                live.remove(gen)


def _rwkv_kernel(r_ref, k_ref, v_ref, lora_ref, gate_ref, w0_ref, a0_ref, kk_ref, ka_ref,
                 rk_ref, lng_ref, lnb_ref, wup_ref, aup_ref, o_ref, s_ref, y_ref, *, n_chunks,
                 n_pairs):
    @pl.when(pl.program_id(2) == 0)
    def _():
        s_ref[...] = jnp.zeros_like(s_ref)

    lane = lax.broadcasted_iota(jnp.int32, (1, LANES), 1)
    head0 = lane < HEAD_SIZE
    rows = lax.broadcasted_iota(jnp.int32, (CHUNK, LANES), 0)
    cols = lax.broadcasted_iota(jnp.int32, (CHUNK, LANES), 1) % HEAD_SIZE
    strict = cols < rows
    incl = cols <= rows
    incl2 = jnp.concatenate([incl, incl], axis=1)
    cr = lax.broadcasted_iota(jnp.int32, (CHUNK, CHUNK), 0)
    cc = lax.broadcasted_iota(jnp.int32, (CHUNK, CHUNK), 1)
    cum_l = _bf((cc <= cr).astype(F32))

    def seg_sum(z):
        s0 = jnp.sum(jnp.where(head0, z, 0.0), axis=-1, keepdims=True)
        s1 = jnp.sum(jnp.where(head0, 0.0, z), axis=-1, keepdims=True)
        return jnp.where(head0, s0, s1)

    def stack(z):
        return _stack(z, head0)

    def pair_transpose(z):
        zero = jnp.zeros_like(z)
        bd = jnp.concatenate([jnp.where(head0, z, zero), jnp.where(head0, zero, z)], axis=0)
        bd_t = bd.T
        return bd_t[:CHUNK] + bd_t[CHUNK:]

    lo = lora_ref[...]
    zw = w0_ref[...] + _mm(_bf(jnp.tanh(lo)), wup_ref[...])
    nz = -zw
    softplus = jnp.maximum(nz, 0.0) + jnp.log(1.0 + jnp.exp(-jnp.abs(nz)))
    w_log = -softplus - 0.5
    lw_all = -jnp.exp(w_log)
    alr_all = jax.nn.sigmoid(a0_ref[...] + _mm(_bf(lo), aup_ref[...]))

    lanes_of = [slice(p * LANES, (p + 1) * LANES) for p in range(n_pairs)]
    rsl = lambda c: slice(c * CHUNK, (c + 1) * CHUNK)
    waves = [list(range(0, n_pairs // 2)), list(range(n_pairs // 2, n_pairs))]
    units_of = [[(p, c) for c in range(n_chunks) for p in wave] for wave in waves]
    g_of = [[_cumsum_rows(cum_l, lw_all[rsl(c), lanes_of[p]]) for p, c in units]
            for units in units_of]
    pre = {}
    bonus = {}

    def prepare(wave):
        r, v, lw, kh, bb, aa = {}, {}, {}, {}, {}, {}
        for p in waves[wave]:
            ls = lanes_of[p]
            r_p = r_ref[:, ls]
            k_p = k_ref[:, ls]
            v_p = v_ref[:, ls]
            alr = alr_all[:, ls]
            kkr = k_p * kk_ref[:, ls]
            kk = kkr * lax.rsqrt(jnp.maximum(seg_sum(kkr * kkr), 1e-24))
            kh_p = k_p * (1.0 + (alr - 1.0) * ka_ref[:, ls])
            r[p], v[p], lw[p], kh[p], bb[p], aa[p] = r_p, v_p, lw_all[:, ls], kh_p, kk * alr, -kk
            bonus[p] = seg_sum(r_p * kh_p * rk_ref[:, ls]) * v_p
            yield
        for (p, c), g in zip(units_of[wave], g_of[wave]):
            lw_u = lw[p][rsl(c)]
            g_end = g[CHUNK - 1:CHUNK, :]
            e_pos = jnp.exp(g)
            e_neg = jnp.exp(-g)
            e_tail = jnp.exp(g_end - g)
            bb_u, kh_u, v_u = bb[p][rsl(c)], kh[p][rsl(c)], v[p][rsl(c)]
            pre[(p, c)] = dict(
                r_t=_bf(r[p][rsl(c)] * e_pos),
                a_t=_bf(aa[p][rsl(c)] * jnp.exp(g - lw_u)),
                b_ts=stack(bb_u * e_neg), k_ts=stack(kh_u * e_neg),
                b_hs=stack(bb_u * e_tail), k_hs=stack(kh_u * e_tail),
                v_s=stack(v_u), v_t=_bf(pair_transpose(v_u)), decay=jnp.exp(g_end))
            yield

    def precompute(wave):
        units = units_of[wave]
        get = lambda name: [pre[unit][name] for unit in units]
        r_t, a_t, b_ts, k_ts, b_hs, k_hs, v_s, v_t = (
            get(name) for name in ("r_t", "a_t", "b_ts", "k_ts", "b_hs", "k_hs", "v_s", "v_t"))
        gram = [_mm_nt(jnp.concatenate([a, rr], axis=0), jnp.concatenate([b, kx], axis=0))
                for a, rr, b, kx in zip(a_t, r_t, b_ts, k_ts)]
        yield
        a_ab = [jnp.where(strict, z[:CHUNK, :LANES], 0.0) for z in gram]
        a_ak = [_bf(jnp.where(strict, z[:CHUNK, LANES:], 0.0)) for z in gram]
        a_r = [_bf(jnp.where(incl2, z[CHUNK:], 0.0)) for z in gram]
        akv = [_mm(a, vs) for a, vs in zip(a_ak, v_s)]
        yield
        t_inv = yield from _unit_lower_inverse(a_ab, rows, cols, head0)
        w12 = [_mm(_bf(t), jnp.concatenate([stack(a), stack(z)], axis=1))
               for t, a, z in zip(t_inv, a_t, akv)]
        yield
        w1 = [z[:, :LANES] for z in w12]
        w2 = [z[:, LANES:] for z in w12]
        w1_b = [_bf(z) for z in w1]
        w1_t = [_bf(pair_transpose(z)) for z in w1]
        w2_t = [_bf(pair_transpose(z)) for z in w2]
        yield
        m_cs = [stack(_mm(wt, b)) for wt, b in zip(w1_t, b_hs)]
        yield
        n_c = [_mm(jnp.concatenate([wt, vx], axis=1), jnp.concatenate([b, kx], axis=0))
               for wt, vx, b, kx in zip(w2_t, v_t, b_hs, k_hs)]
        for unit, ar, wb, w2u, mc, nc in zip(units, a_r, w1_b, w2, m_cs, n_c):
            pre[unit].update(a_r=ar, w1_b=wb, w2=w2u, m_cs=mc, n_c=nc)
        yield

    def chain(wave):
        s = {p: s_ref[p] for p in waves[wave]}
        us_prev = {}
        for c in range(n_chunks + 1):
            us_now = {}
            if c < n_chunks:
                for p in waves[wave]:
                    q = pre[(p, c)]
                    s_b = _bf(s[p])
                    s_s = stack(s[p])
                    s[p] = s[p] * q["decay"] + _mm(s_b, q["m_cs"]) + q["n_c"]
                    us_now[p] = _mm_nt(jnp.concatenate([q["w1_b"], q["r_t"]], axis=0), s_s)
            if c > 0:
                for p in waves[wave]:
                    q = pre[(p, c - 1)]
                    u = us_prev[p][:CHUNK] + q["w2"]
                    y_ref[rsl(c - 1), lanes_of[p]] = us_prev[p][CHUNK:] + _mm(
                        q["a_r"], jnp.concatenate([stack(u), q["v_s"]], axis=0))
            us_prev = us_now
            yield
        for p in waves[wave]:
            s_ref[p] = s[p]

    _interleave(prepare(0))
    _interleave(precompute(0), prepare(1))
    _interleave(precompute(1), chain(0))
    _interleave(chain(1))

    inv_n = 1.0 / HEAD_SIZE
    for p, ls in enumerate(lanes_of):
        y = y_ref[:, ls]
        mu = seg_sum(y) * inv_n
        d = y - mu
        var = seg_sum(d * d) * inv_n
        yn = d * lax.rsqrt(var + GN_EPS) * lng_ref[:, ls] + lnb_ref[:, ls]
        o_ref[:, ls] = ((yn + bonus[p]) * gate_ref[:, ls]).astype(o_ref.dtype)


def _rwkv(p3, gate3, w0, a0, k_k, k_a, r_k, lnx_g, lnx_b, wup, aup, rwkv_w, tt=128):
    b, t, _ = p3.shape
    width = PAIRS_PER_STEP * LANES
    n_steps = rwkv_w // width
    lora_blk = (3 * rwkv_w) // LORA_PAD
    vec = lambda z: z.reshape(1, rwkv_w).astype(F32)
    col = lambda off: pl.BlockSpec((None, tt, width), lambda bi, hp, ti: (bi, ti, off + hp))
    par = pl.BlockSpec((1, width), lambda bi, hp, ti: (0, hp))
    up = pl.BlockSpec((LORA_PAD, width), lambda bi, hp, ti: (0, hp))
    kern = functools.partial(_rwkv_kernel, n_chunks=tt // CHUNK, n_pairs=PAIRS_PER_STEP)
    return pl.pallas_call(
        kern,
        out_shape=jax.ShapeDtypeStruct((b, t, rwkv_w), BF16),
        grid=(b, n_steps, t // tt),
        in_specs=[col(0), col(n_steps), col(2 * n_steps),
                  pl.BlockSpec((None, tt, LORA_PAD), lambda bi, hp, ti: (bi, ti, lora_blk)),
                  col(0),
                  par, par, par, par, par, par, par, up, up],
        out_specs=col(0),
        scratch_shapes=[pltpu.VMEM((PAIRS_PER_STEP, CHUNK, LANES), F32),
                        pltpu.VMEM((tt, width), F32)],
        compiler_params=_cparams(("parallel", "parallel", "arbitrary")),
        name="rwkv7",
    )(p3, p3, p3, p3, gate3, vec(w0), vec(a0), vec(k_k), vec(k_a), vec(r_k), vec(lnx_g),
      vec(lnx_b), wup, aup)


HALO = 32
CONV_ROWS = 64
CONV_LANES = 128


def _conv_kernel(u_ref, gate_ref, cw_ref, cb_ref, lg_ref, lb_ref, w_ref, b_ref, o_ref, buf_ref,
                 acc_ref, *, tt):
    cw_all = buf_ref.shape[1]

    @pl.when(pl.program_id(1) == 0)
    def _():
        buf_ref[0:HALO, :] = jnp.zeros((HALO, cw_all), F32)

    buf_ref[HALO:HALO + tt, :] = u_ref[...]
    base = HALO - (CONV_K - 1)

    win_rows = CONV_ROWS + HALO

    def lane_block(lb, carry):
        ls = pl.ds(pl.multiple_of(lb * CONV_LANES, CONV_LANES), CONV_LANES)
        bias = cb_ref[:, ls]
        for rb in range(tt // CONV_ROWS):
            win = buf_ref[rb * CONV_ROWS:rb * CONV_ROWS + win_rows, ls]
            acc = jnp.zeros((CONV_ROWS, CONV_LANES), F32) + bias
            for res in range(SUBLANES):
                sub = (base + res) % SUBLANES
                rot = win if sub == 0 else pltpu.roll(win, win_rows - sub, axis=0)
                for j in range(res, CONV_K, SUBLANES):
                    lo_row = base + j - sub
                    acc = acc + rot[lo_row:lo_row + CONV_ROWS] * cw_ref[pl.ds(j, 1), ls]
            acc_ref[rb * CONV_ROWS:(rb + 1) * CONV_ROWS, ls] = acc
        return carry

    lax.fori_loop(0, cw_all // CONV_LANES, lane_block, 0)
    tail = buf_ref[tt:tt + HALO, :]
    buf_ref[0:HALO, :] = tail

    acc = acc_ref[...]
    mu = jnp.mean(acc, axis=-1, keepdims=True)
    d = acc - mu
    var = jnp.mean(d * d, axis=-1, keepdims=True)
    c = d * lax.rsqrt(var + LN_EPS) * lg_ref[...] + lb_ref[...]
    c = c * jax.nn.sigmoid(c)
    z = jnp.dot(c.astype(BF16), w_ref[...], preferred_element_type=F32) + b_ref[...]
    o_ref[...] = (z * gate_ref[...]).astype(o_ref.dtype)


def _conv(u3, gate3, conv_w, conv_b, cln_g, cln_b, w_pw2, b_pw2, tt=256):
    b, t, cw = u3.shape
    vec = lambda z: z.reshape(1, cw).astype(F32)
    par = pl.BlockSpec((1, cw), lambda bi, ti: (0, 0))
    row = pl.BlockSpec((None, tt, cw), lambda bi, ti: (bi, ti, 0))
    kern = functools.partial(_conv_kernel, tt=tt)
    return pl.pallas_call(
        kern,
        out_shape=jax.ShapeDtypeStruct((b, t, cw), BF16),
        grid=(b, t // tt),
        in_specs=[row, row,
                  pl.BlockSpec((CONV_K, cw), lambda bi, ti: (0, 0)),
                  par, par, par,
                  pl.BlockSpec((cw, cw), lambda bi, ti: (0, 0)),
                  par],
        out_specs=row,
        scratch_shapes=[pltpu.VMEM((HALO + tt, cw), F32), pltpu.VMEM((tt, cw), F32)],
        compiler_params=_cparams(("parallel", "arbitrary")),
        name="conv_group",
    )(u3, gate3, conv_w.astype(F32), vec(conv_b), vec(cln_g), vec(cln_b), w_pw2.astype(BF16),
      vec(b_pw2))


def _outproj_kernel(ya_ref, yb_ref, wa_ref, wb_ref, x_ref, g_ref, o_ref, *, tn, n_steps):
    j = pl.program_id(1)
    acc = jnp.dot(ya_ref[...], wa_ref[...], preferred_element_type=F32)
    acc = acc + jnp.dot(yb_ref[...], wb_ref[...], preferred_element_type=F32)
    o_ref[:, pl.ds(pl.multiple_of(j * tn, tn), tn)] = acc

    @pl.when(j == n_steps - 1)
    def _():
        out = o_ref[...]
        ms = jnp.mean(out * out, axis=-1, keepdims=True)
        o_ref[...] = x_ref[...] + out * lax.rsqrt(ms + NORM_EPS) * g_ref[...]


def _outproj(ya, yb, w, x2, g, tm=512, tn=512):
    m, ka = ya.shape
    kb = yb.shape[1]
    d = w.shape[1]
    kern = functools.partial(_outproj_kernel, tn=tn, n_steps=d // tn)
    return pl.pallas_call(
        kern,
        out_shape=jax.ShapeDtypeStruct((m, d), F32),
        grid=(m // tm, d // tn),
        in_specs=[pl.BlockSpec((tm, ka), lambda i, j: (i, 0)),
                  pl.BlockSpec((tm, kb), lambda i, j: (i, 0)),
                  pl.BlockSpec((ka, tn), lambda i, j: (0, j)),
                  pl.BlockSpec((kb, tn), lambda i, j: (ka // kb, j)),
                  pl.BlockSpec((tm, d), lambda i, j: (i, 0)),
                  pl.BlockSpec((1, d), lambda i, j: (0, 0))],
        out_specs=pl.BlockSpec((tm, d), lambda i, j: (i, 0)),
        compiler_params=_cparams(("parallel", "arbitrary")),
        name="outproj_norm",
    )(ya, yb, w, w, x2, g.reshape(1, d).astype(F32))


def kernel(x, norm_pre_g, w_in, mu_shift, w0, w_lora_up, a0, a_lora_up, k_k, k_a, r_k, lnx_g,
           lnx_b, conv_w, conv_b, cln_g, cln_b, w_pw2, b_pw2, w_out, norm_post_g):
    bsz, seq, d_model = x.shape
    rwkv_w = w0.shape[0]
    conv_wd = conv_b.shape[0]
    assert rwkv_w == conv_wd
    shift_cols = 3 * rwkv_w + LORA_W + LORA_A
    shift_pad = 3 * rwkv_w + LORA_PAD
    m = bsz * seq
    x2 = x.reshape(m, d_model)

    pad = shift_pad - shift_cols
    assert shift_cols % WCAST_PIECE == 0 and w_in.shape[1] % WCAST_PIECE == 0
    wt3 = w_in.T.reshape(w_in.shape[1] // WCAST_PIECE, WCAST_PIECE, d_model)
    w_shift = _wcast(wt3, 0, shift_pad)
    w_rest = _wcast(wt3, shift_cols // WCAST_PIECE, w_in.shape[1] - shift_cols)
    mu_p = jnp.concatenate([mu_shift, jnp.zeros((pad,), mu_shift.dtype)]).reshape(1, -1).astype(F32)
    wup = jnp.zeros((LORA_PAD, rwkv_w), BF16).at[:LORA_W].set(w_lora_up.astype(BF16))
    aup = jnp.zeros((LORA_PAD, rwkv_w), BF16).at[LORA_W:LORA_W + LORA_A].set(a_lora_up.astype(BF16))

    h = _prenorm(x2, norm_pre_g.astype(F32))
    p = _mm_shift(h, w_shift, mu_p, seq)
    gate_r, gate_c, u = _mm_gates(h, w_rest, rwkv_w)

    to3 = lambda z: z.reshape(bsz, seq, -1)
    y_rwkv = _rwkv(to3(p), to3(gate_r), w0, a0, k_k, k_a, r_k, lnx_g, lnx_b, wup, aup, rwkv_w)
    y_conv = _conv(to3(u), to3(gate_c), conv_w, conv_b, cln_g, cln_b, w_pw2, b_pw2)

    out = _outproj(y_rwkv.reshape(m, rwkv_w), y_conv.reshape(m, conv_wd), w_out.astype(BF16), x2,
                   norm_post_g)
    return out.reshape(bsz, seq, d_model)
```

```python
import functools

import jax
import jax.numpy as jnp
from jax import lax
from jax.experimental import pallas as pl
from jax.experimental.pallas import tpu as pltpu

F32 = jnp.float32
BF16 = jnp.bfloat16

HEAD_SIZE = 64
LORA_W = 96
LORA_A = 96
CONV_K = 31
NORM_EPS = 1e-6
LN_EPS = 1e-5
GN_EPS = 1e-5 * HEAD_SIZE

LANES = 128
SUBLANES = 8
LORA_PAD = 256
CHUNK = 64
PAIRS_PER_STEP = 16
VMEM_LIMIT = 56 * 1024 * 1024

assert CHUNK == HEAD_SIZE


def _cparams(sem):
    return pltpu.CompilerParams(dimension_semantics=sem, vmem_limit_bytes=VMEM_LIMIT)


def _prenorm_kernel(x_ref, g_ref, o_ref):
    x = x_ref[...]
    ms = jnp.mean(x * x, axis=-1, keepdims=True)
    o_ref[...] = (x * lax.rsqrt(ms + NORM_EPS) * g_ref[...]).astype(o_ref.dtype)


def _prenorm(x2, g, tm=512):
    m, d = x2.shape
    return pl.pallas_call(
        _prenorm_kernel,
        out_shape=jax.ShapeDtypeStruct((m, d), BF16),
        grid=(m // tm,),
        in_specs=[pl.BlockSpec((tm, d), lambda i: (i, 0)),
                  pl.BlockSpec((1, d), lambda i: (0, 0))],
        out_specs=pl.BlockSpec((tm, d), lambda i: (i, 0)),
        compiler_params=_cparams(("parallel",)),
        name="prenorm",
    )(x2, g.reshape(1, d))


WCAST_PIECE = 64


def _wcast_kernel(a_ref, b_ref, o_ref):
    rows = jnp.concatenate([a_ref[...], b_ref[...]], axis=0)
    o_ref[...] = rows.T.astype(o_ref.dtype)


def _wcast(wt3, piece0, n_cols):
    k = wt3.shape[2]
    piece = lambda off: pl.BlockSpec((None, WCAST_PIECE, k), lambda i: (piece0 + 2 * i + off, 0, 0))
    return pl.pallas_call(
        _wcast_kernel,
        out_shape=jax.ShapeDtypeStruct((k, n_cols), BF16),
        grid=(n_cols // LANES,),
        in_specs=[piece(0), piece(1)],
        out_specs=pl.BlockSpec((k, LANES), lambda i: (0, i)),
        compiler_params=_cparams(("parallel",)),
        name="weight_cast",
    )(wt3, wt3)


def _mm_shift_kernel(h_ref, w_ref, mu_ref, o_ref, carry_ref, *, tiles_per_seq):
    i = pl.program_id(1)
    p = jnp.dot(h_ref[...], w_ref[...], preferred_element_type=F32)

    @pl.when(i % tiles_per_seq == 0)
    def _():
        carry_ref[...] = jnp.zeros_like(carry_ref)

    rows = lax.broadcasted_iota(jnp.int32, p.shape, 0)
    prev = jnp.where(rows == 0, carry_ref[...], pltpu.roll(p, 1, axis=0))
    carry_ref[...] = p[-1:, :]
    o_ref[...] = p + (prev - p) * mu_ref[...]


def _mm_shift(h, w, mu, seq_len, tm=1024, tn=1280):
    m, k = h.shape
    n = w.shape[1]
    tm = min(tm, seq_len)
    kern = functools.partial(_mm_shift_kernel, tiles_per_seq=seq_len // tm)
    return pl.pallas_call(
        kern,
        out_shape=jax.ShapeDtypeStruct((m, n), F32),
        grid=(n // tn, m // tm),
        in_specs=[pl.BlockSpec((tm, k), lambda j, i: (i, 0)),
                  pl.BlockSpec((k, tn), lambda j, i: (0, j), pipeline_mode=pl.Buffered(1)),
                  pl.BlockSpec((1, tn), lambda j, i: (0, j))],
        out_specs=pl.BlockSpec((tm, tn), lambda j, i: (i, j)),
        scratch_shapes=[pltpu.VMEM((1, tn), F32)],
        compiler_params=_cparams(("parallel", "arbitrary")),
        name="inproj_shift",
    )(h, w, mu)


def _mm_gates_kernel(h_ref, wgr_ref, wgv_ref, wgg_ref, wgc_ref, gr_ref, gc_ref, u_ref):
    h = h_ref[...]
    dot = lambda w_ref: jnp.dot(h, w_ref[...], preferred_element_type=F32)
    pr = dot(wgr_ref)
    gr_ref[...] = pr * jax.nn.sigmoid(pr)
    pc = dot(wgc_ref)
    gc_ref[...] = pc * jax.nn.sigmoid(pc)
    u_ref[...] = dot(wgv_ref) * jax.nn.sigmoid(dot(wgg_ref))


def _mm_gates(h, w, width, tm=1024, tn=256):
    m, k = h.shape
    tm = min(tm, m)
    per = width // tn
    wspec = lambda g: pl.BlockSpec((k, tn), lambda i, j: (0, g * per + j))
    ospec = pl.BlockSpec((tm, tn), lambda i, j: (i, j))
    oshape = jax.ShapeDtypeStruct((m, width), F32)
    return pl.pallas_call(
        _mm_gates_kernel,
        out_shape=(oshape, oshape, oshape),
        grid=(m // tm, per),
        in_specs=[pl.BlockSpec((tm, k), lambda i, j: (i, 0)), wspec(0), wspec(1), wspec(2), wspec(3)],
        out_specs=(ospec, ospec, ospec),
        compiler_params=_cparams(("parallel", "arbitrary")),
        name="inproj_gates",
    )(h, w, w, w, w)


def _bf(z):
    return z.astype(BF16)


def _mm(a, b):
    return jnp.dot(a, b, preferred_element_type=F32)


def _mm_nt(a, b):
    return lax.dot_general(a, b, (((1,), (1,)), ((), ())), preferred_element_type=F32)


def _cumsum_rows(cum_l, z):
    hi = _bf(z)
    r1 = z - hi.astype(F32)
    mid = _bf(r1)
    lo = _bf(r1 - mid.astype(F32))
    two = _mm(cum_l, jnp.concatenate([hi, mid], axis=1))
    return two[:, :LANES] + two[:, LANES:] + _mm(cum_l, lo)


def _stack(z, head0):
    zb = _bf(z)
    zero = jnp.zeros_like(zb)
    return jnp.concatenate([jnp.where(head0, zb, zero), jnp.where(head0, zero, zb)], axis=0)


def _unit_lower_inverse(a_list, rows, cols, head0):
    eye = (rows == cols).astype(F32)
    same8 = (rows // 8) == (cols // 8)
    a8 = [jnp.where(same8, a, 0.0) for a in a_list]
    a8_s = [_stack(z, head0) for z in a8]
    a8_2 = [_mm(_bf(z), zs) for z, zs in zip(a8, a8_s)]
    yield
    a8_2s = [_stack(z, head0) for z in a8_2]
    a8_4s = [_stack(_mm(_bf(z), zs), head0) for z, zs in zip(a8_2, a8_2s)]
    yield
    p1 = [eye + z for z in a8]
    p1 = [p + _mm(_bf(p), zs) for p, zs in zip(p1, a8_2s)]
    yield
    t = [p + _mm(_bf(p), zs) for p, zs in zip(p1, a8_4s)]
    yield
    blk = 8
    while blk < CHUNK:
        inner = (rows // blk) == (cols // blk)
        outer = (rows // (2 * blk)) == (cols // (2 * blk))
        join = outer & jnp.logical_not(inner)
        e_s = [_stack(jnp.where(join, a, 0.0), head0) for a in a_list]
        t_b = [_bf(z) for z in t]
        t_s = [_stack(z, head0) for z in t]
        te = [_bf(_mm(tb, es)) for tb, es in zip(t_b, e_s)]
        yield
        t = [z + _mm(tez, ts) for z, tez, ts in zip(t, te, t_s)]
        yield
        blk *= 2
    return t


def _interleave(*gens):
    live = list(gens)
    while live:
        for gen in list(live):
            try:
                next(gen)
            except StopIteration:
                live.remove(gen)


def _rwkv_kernel(r_ref, k_ref, v_ref, lora_ref, gate_ref, w0_ref, a0_ref, kk_ref, ka_ref,
                 rk_ref, lng_ref, lnb_ref, wup_ref, aup_ref, o_ref, s_ref, y_ref, *, n_chunks,
                 n_pairs):
    @pl.when(pl.program_id(2) == 0)
    def _():
        s_ref[...] = jnp.zeros_like(s_ref)

    lane = lax.broadcasted_iota(jnp.int32, (1, LANES), 1)
    head0 = lane < HEAD_SIZE
    rows = lax.broadcasted_iota(jnp.int32, (CHUNK, LANES), 0)
    cols = lax.broadcasted_iota(jnp.int32, (CHUNK, LANES), 1) % HEAD_SIZE
    strict = cols < rows
    incl = cols <= rows
    incl2 = jnp.concatenate([incl, incl], axis=1)
    cr = lax.broadcasted_iota(jnp.int32, (CHUNK, CHUNK), 0)
    cc = lax.broadcasted_iota(jnp.int32, (CHUNK, CHUNK), 1)
    cum_l = _bf((cc <= cr).astype(F32))

    def seg_sum(z):
        s0 = jnp.sum(jnp.where(head0, z, 0.0), axis=-1, keepdims=True)
        s1 = jnp.sum(jnp.where(head0, 0.0, z), axis=-1, keepdims=True)
        return jnp.where(head0, s0, s1)

    def stack(z):
        return _stack(z, head0)

    def pair_transpose(z):
        zero = jnp.zeros_like(z)
        bd = jnp.concatenate([jnp.where(head0, z, zero), jnp.where(head0, zero, z)], axis=0)
        bd_t = bd.T
        return bd_t[:CHUNK] + bd_t[CHUNK:]

    lo = lora_ref[...]
    zw = w0_ref[...] + _mm(_bf(jnp.tanh(lo)), wup_ref[...])
    nz = -zw
    softplus = jnp.maximum(nz, 0.0) + jnp.log(1.0 + jnp.exp(-jnp.abs(nz)))
    w_log = -softplus - 0.5
    lw_all = -jnp.exp(w_log)
    alr_all = jax.nn.sigmoid(a0_ref[...] + _mm(_bf(lo), aup_ref[...]))

    lanes_of = [slice(p * LANES, (p + 1) * LANES) for p in range(n_pairs)]
    rsl = lambda c: slice(c * CHUNK, (c + 1) * CHUNK)
    waves = [list(range(0, n_pairs // 2)), list(range(n_pairs // 2, n_pairs))]
    units_of = [[(p, c) for c in range(n_chunks) for p in wave] for wave in waves]
    g_of = [[_cumsum_rows(cum_l, lw_all[rsl(c), lanes_of[p]]) for p, c in units]
            for units in units_of]
    pre = {}
    bonus = {}

    def prepare(wave):
        r, v, lw, kh, bb, aa = {}, {}, {}, {}, {}, {}
        for p in waves[wave]:
            ls = lanes_of[p]
            r_p = r_ref[:, ls]
            k_p = k_ref[:, ls]
            v_p = v_ref[:, ls]
            alr = alr_all[:, ls]
            kkr = k_p * kk_ref[:, ls]
            kk = kkr * lax.rsqrt(jnp.maximum(seg_sum(kkr * kkr), 1e-24))
            kh_p = k_p * (1.0 + (alr - 1.0) * ka_ref[:, ls])
            r[p], v[p], lw[p], kh[p], bb[p], aa[p] = r_p, v_p, lw_all[:, ls], kh_p, kk * alr, -kk
            bonus[p] = seg_sum(r_p * kh_p * rk_ref[:, ls]) * v_p
            yield
        for (p, c), g in zip(units_of[wave], g_of[wave]):
            lw_u = lw[p][rsl(c)]
            g_end = g[CHUNK - 1:CHUNK, :]
            e_pos = jnp.exp(g)
            e_neg = jnp.exp(-g)
            e_tail = jnp.exp(g_end - g)
            bb_u, kh_u, v_u = bb[p][rsl(c)], kh[p][rsl(c)], v[p][rsl(c)]
            pre[(p, c)] = dict(
                r_t=_bf(r[p][rsl(c)] * e_pos),
                a_t=_bf(aa[p][rsl(c)] * jnp.exp(g - lw_u)),
                b_ts=stack(bb_u * e_neg), k_ts=stack(kh_u * e_neg),
                b_hs=stack(bb_u * e_tail), k_hs=stack(kh_u * e_tail),
                v_s=stack(v_u), v_t=_bf(pair_transpose(v_u)), decay=jnp.exp(g_end))
            yield

    def precompute(wave):
        units = units_of[wave]
        get = lambda name: [pre[unit][name] for unit in units]
        r_t, a_t, b_ts, k_ts, b_hs, k_hs, v_s, v_t = (
            get(name) for name in ("r_t", "a_t", "b_ts", "k_ts", "b_hs", "k_hs", "v_s", "v_t"))
        gram = [_mm_nt(jnp.concatenate([a, rr], axis=0), jnp.concatenate([b, kx], axis=0))
                for a, rr, b, kx in zip(a_t, r_t, b_ts, k_ts)]
        yield
        a_ab = [jnp.where(strict, z[:CHUNK, :LANES], 0.0) for z in gram]
        a_ak = [_bf(jnp.where(strict, z[:CHUNK, LANES:], 0.0)) for z in gram]
        a_r = [_bf(jnp.where(incl2, z[CHUNK:], 0.0)) for z in gram]
        akv = [_mm(a, vs) for a, vs in zip(a_ak, v_s)]
        yield
        t_inv = yield from _unit_lower_inverse(a_ab, rows, cols, head0)
        w12 = [_mm(_bf(t), jnp.concatenate([stack(a), stack(z)], axis=1))
               for t, a, z in zip(t_inv, a_t, akv)]
        yield
        w1 = [z[:, :LANES] for z in w12]
        w2 = [z[:, LANES:] for z in w12]
        w1_b = [_bf(z) for z in w1]
        w1_t = [_bf(pair_transpose(z)) for z in w1]
        w2_t = [_bf(pair_transpose(z)) for z in w2]
        yield
        m_cs = [stack(_mm(wt, b)) for wt, b in zip(w1_t, b_hs)]
        yield
        n_c = [_mm(jnp.concatenate([wt, vx], axis=1), jnp.concatenate([b, kx], axis=0))
               for wt, vx, b, kx in zip(w2_t, v_t, b_hs, k_hs)]
        for unit, ar, wb, w2u, mc, nc in zip(units, a_r, w1_b, w2, m_cs, n_c):
            pre[unit].update(a_r=ar, w1_b=wb, w2=w2u, m_cs=mc, n_c=nc)
        yield

    def chain(wave):
        s = {p: s_ref[p] for p in waves[wave]}
        us_prev = {}
        for c in range(n_chunks + 1):
            us_now = {}
            if c < n_chunks:
                for p in waves[wave]:
                    q = pre[(p, c)]
                    s_b = _bf(s[p])
                    s_s = stack(s[p])
                    s[p] = s[p] * q["decay"] + _mm(s_b, q["m_cs"]) + q["n_c"]
                    us_now[p] = _mm_nt(jnp.concatenate([q["w1_b"], q["r_t"]], axis=0), s_s)
            if c > 0:
                for p in waves[wave]:
                    q = pre[(p, c - 1)]
                    u = us_prev[p][:CHUNK] + q["w2"]
                    y_ref[rsl(c - 1), lanes_of[p]] = us_prev[p][CHUNK:] + _mm(
                        q["a_r"], jnp.concatenate([stack(u), q["v_s"]], axis=0))
            us_prev = us_now
            yield
        for p in waves[wave]:
            s_ref[p] = s[p]

    _interleave(prepare(0))
    _interleave(precompute(0), prepare(1))
    _interleave(precompute(1), chain(0))
    _interleave(chain(1))

    inv_n = 1.0 / HEAD_SIZE
    for p, ls in enumerate(lanes_of):
        y = y_ref[:, ls]
        mu = seg_sum(y) * inv_n
        d = y - mu
        var = seg_sum(d * d) * inv_n
        yn = d * lax.rsqrt(var + GN_EPS) * lng_ref[:, ls] + lnb_ref[:, ls]
        o_ref[:, ls] = ((yn + bonus[p]) * gate_ref[:, ls]).astype(o_ref.dtype)


def _rwkv(p3, gate3, w0, a0, k_k, k_a, r_k, lnx_g, lnx_b, wup, aup, rwkv_w, tt=128):
    b, t, _ = p3.shape
    width = PAIRS_PER_STEP * LANES
    n_steps = rwkv_w // width
    lora_blk = (3 * rwkv_w) // LORA_PAD
    vec = lambda z: z.reshape(1, rwkv_w).astype(F32)
    col = lambda off: pl.BlockSpec((None, tt, width), lambda bi, hp, ti: (bi, ti, off + hp))
    par = pl.BlockSpec((1, width), lambda bi, hp, ti: (0, hp))
    up = pl.BlockSpec((LORA_PAD, width), lambda bi, hp, ti: (0, hp))
    kern = functools.partial(_rwkv_kernel, n_chunks=tt // CHUNK, n_pairs=PAIRS_PER_STEP)
    return pl.pallas_call(
        kern,
        out_shape=jax.ShapeDtypeStruct((b, t, rwkv_w), BF16),
        grid=(b, n_steps, t // tt),
        in_specs=[col(0), col(n_steps), col(2 * n_steps),
                  pl.BlockSpec((None, tt, LORA_PAD), lambda bi, hp, ti: (bi, ti, lora_blk)),
                  col(0),
                  par, par, par, par, par, par, par, up, up],
        out_specs=col(0),
        scratch_shapes=[pltpu.VMEM((PAIRS_PER_STEP, CHUNK, LANES), F32),
                        pltpu.VMEM((tt, width), F32)],
        compiler_params=_cparams(("parallel", "parallel", "arbitrary")),
        name="rwkv7",
    )(p3, p3, p3, p3, gate3, vec(w0), vec(a0), vec(k_k), vec(k_a), vec(r_k), vec(lnx_g),
      vec(lnx_b), wup, aup)


HALO = 32
CONV_ROWS = 64
CONV_LANES = 128


def _conv_kernel(u_ref, gate_ref, cw_ref, cb_ref, lg_ref, lb_ref, w_ref, b_ref, o_ref, buf_ref,
                 acc_ref, *, tt):
    cw_all = buf_ref.shape[1]

    @pl.when(pl.program_id(1) == 0)
    def _():
        buf_ref[0:HALO, :] = jnp.zeros((HALO, cw_all), F32)

    buf_ref[HALO:HALO + tt, :] = u_ref[...]
    base = HALO - (CONV_K - 1)

    win_rows = CONV_ROWS + HALO

    def lane_block(lb, carry):
        ls = pl.ds(pl.multiple_of(lb * CONV_LANES, CONV_LANES), CONV_LANES)
        bias = cb_ref[:, ls]
        for rb in range(tt // CONV_ROWS):
            win = buf_ref[rb * CONV_ROWS:rb * CONV_ROWS + win_rows, ls]
            acc = jnp.zeros((CONV_ROWS, CONV_LANES), F32) + bias
            for res in range(SUBLANES):
                sub = (base + res) % SUBLANES
                rot = win if sub == 0 else pltpu.roll(win, win_rows - sub, axis=0)
                for j in range(res, CONV_K, SUBLANES):
                    lo_row = base + j - sub
                    acc = acc + rot[lo_row:lo_row + CONV_ROWS] * cw_ref[pl.ds(j, 1), ls]
            acc_ref[rb * CONV_ROWS:(rb + 1) * CONV_ROWS, ls] = acc
        return carry

    lax.fori_loop(0, cw_all // CONV_LANES, lane_block, 0)
    tail = buf_ref[tt:tt + HALO, :]
    buf_ref[0:HALO, :] = tail

    acc = acc_ref[...]
    mu = jnp.mean(acc, axis=-1, keepdims=True)
    d = acc - mu
    var = jnp.mean(d * d, axis=-1, keepdims=True)
    c = d * lax.rsqrt(var + LN_EPS) * lg_ref[...] + lb_ref[...]
    c = c * jax.nn.sigmoid(c)
    z = jnp.dot(c.astype(BF16), w_ref[...], preferred_element_type=F32) + b_ref[...]
    o_ref[...] = (z * gate_ref[...]).astype(o_ref.dtype)


def _conv(u3, gate3, conv_w, conv_b, cln_g, cln_b, w_pw2, b_pw2, tt=256):
    b, t, cw = u3.shape
    vec = lambda z: z.reshape(1, cw).astype(F32)
    par = pl.BlockSpec((1, cw), lambda bi, ti: (0, 0))
    row = pl.BlockSpec((None, tt, cw), lambda bi, ti: (bi, ti, 0))
    kern = functools.partial(_conv_kernel, tt=tt)
    return pl.pallas_call(
        kern,
        out_shape=jax.ShapeDtypeStruct((b, t, cw), BF16),
        grid=(b, t // tt),
        in_specs=[row, row,
                  pl.BlockSpec((CONV_K, cw), lambda bi, ti: (0, 0)),
                  par, par, par,
                  pl.BlockSpec((cw, cw), lambda bi, ti: (0, 0)),
                  par],
        out_specs=row,
        scratch_shapes=[pltpu.VMEM((HALO + tt, cw), F32), pltpu.VMEM((tt, cw), F32)],
        compiler_params=_cparams(("parallel", "arbitrary")),
        name="conv_group",
    )(u3, gate3, conv_w.astype(F32), vec(conv_b), vec(cln_g), vec(cln_b), w_pw2.astype(BF16),
      vec(b_pw2))


def _outproj_kernel(ya_ref, yb_ref, wa_ref, wb_ref, x_ref, g_ref, o_ref, *, tn, n_steps):
    j = pl.program_id(1)
    acc = jnp.dot(ya_ref[...], wa_ref[...], preferred_element_type=F32)
    acc = acc + jnp.dot(yb_ref[...], wb_ref[...], preferred_element_type=F32)
    o_ref[:, pl.ds(pl.multiple_of(j * tn, tn), tn)] = acc

    @pl.when(j == n_steps - 1)
    def _():
        out = o_ref[...]
        ms = jnp.mean(out * out, axis=-1, keepdims=True)
        o_ref[...] = x_ref[...] + out * lax.rsqrt(ms + NORM_EPS) * g_ref[...]


def _outproj(ya, yb, w, x2, g, tm=512, tn=512):
    m, ka = ya.shape
    kb = yb.shape[1]
    d = w.shape[1]
    kern = functools.partial(_outproj_kernel, tn=tn, n_steps=d // tn)
    return pl.pallas_call(
        kern,
        out_shape=jax.ShapeDtypeStruct((m, d), F32),
        grid=(m // tm, d // tn),
        in_specs=[pl.BlockSpec((tm, ka), lambda i, j: (i, 0)),
                  pl.BlockSpec((tm, kb), lambda i, j: (i, 0)),
                  pl.BlockSpec((ka, tn), lambda i, j: (0, j)),
                  pl.BlockSpec((kb, tn), lambda i, j: (ka // kb, j)),
                  pl.BlockSpec((tm, d), lambda i, j: (i, 0)),
                  pl.BlockSpec((1, d), lambda i, j: (0, 0))],
        out_specs=pl.BlockSpec((tm, d), lambda i, j: (i, 0)),
        compiler_params=_cparams(("parallel", "arbitrary")),
        name="outproj_norm",
    )(ya, yb, w, w, x2, g.reshape(1, d).astype(F32))


def kernel(x, norm_pre_g, w_in, mu_shift, w0, w_lora_up, a0, a_lora_up, k_k, k_a, r_k, lnx_g,
           lnx_b, conv_w, conv_b, cln_g, cln_b, w_pw2, b_pw2, w_out, norm_post_g):
    bsz, seq, d_model = x.shape
    rwkv_w = w0.shape[0]
    conv_wd = conv_b.shape[0]
    assert rwkv_w == conv_wd
    shift_cols = 3 * rwkv_w + LORA_W + LORA_A
    shift_pad = 3 * rwkv_w + LORA_PAD
    m = bsz * seq
    x2 = x.reshape(m, d_model)

    pad = shift_pad - shift_cols
    assert shift_cols % WCAST_PIECE == 0 and w_in.shape[1] % WCAST_PIECE == 0
    wt3 = w_in.T.reshape(w_in.shape[1] // WCAST_PIECE, WCAST_PIECE, d_model)
    w_shift = _wcast(wt3, 0, shift_pad)
    w_rest = _wcast(wt3, shift_cols // WCAST_PIECE, w_in.shape[1] - shift_cols)
    mu_p = jnp.concatenate([mu_shift, jnp.zeros((pad,), mu_shift.dtype)]).reshape(1, -1).astype(F32)
    wup = jnp.zeros((LORA_PAD, rwkv_w), BF16).at[:LORA_W].set(w_lora_up.astype(BF16))
    aup = jnp.zeros((LORA_PAD, rwkv_w), BF16).at[LORA_W:LORA_W + LORA_A].set(a_lora_up.astype(BF16))

    h = _prenorm(x2, norm_pre_g.astype(F32))
    p = _mm_shift(h, w_shift, mu_p, seq)
    gate_r, gate_c, u = _mm_gates(h, w_rest, rwkv_w)

    to3 = lambda z: z.reshape(bsz, seq, -1)
    y_rwkv = _rwkv(to3(p), to3(gate_r), w0, a0, k_k, k_a, r_k, lnx_g, lnx_b, wup, aup, rwkv_w)
    y_conv = _conv(to3(u), to3(gate_c), conv_w, conv_b, cln_g, cln_b, w_pw2, b_pw2)

    out = _outproj(y_rwkv.reshape(m, rwkv_w), y_conv.reshape(m, conv_wd), w_out.astype(BF16), x2,
                   norm_post_g)
    return out.reshape(bsz, seq, d_model)
```

```python
import functools

import jax
import jax.numpy as jnp
from jax import lax
from jax.experimental import pallas as pl
from jax.experimental.pallas import tpu as pltpu

F32 = jnp.float32
BF16 = jnp.bfloat16

HEAD_SIZE = 64
LORA_W = 96
LORA_A = 96
CONV_K = 31
NORM_EPS = 1e-6
LN_EPS = 1e-5
GN_EPS = 1e-5 * HEAD_SIZE

LANES = 128
SUBLANES = 8
LORA_PAD = 256
CHUNK = 64
INV_BASE = 8
PAIRS_PER_STEP = 16
VMEM_LIMIT = 56 * 1024 * 1024

assert CHUNK == HEAD_SIZE


def _cparams(sem):
    return pltpu.CompilerParams(dimension_semantics=sem, vmem_limit_bytes=VMEM_LIMIT)


def _prenorm_kernel(x_ref, g_ref, o_ref):
    x = x_ref[...]
    ms = jnp.mean(x * x, axis=-1, keepdims=True)
    o_ref[...] = (x * lax.rsqrt(ms + NORM_EPS) * g_ref[...]).astype(o_ref.dtype)


def _prenorm(x2, g, tm=256):
    m, d = x2.shape
    return pl.pallas_call(
        _prenorm_kernel,
        out_shape=jax.ShapeDtypeStruct((m, d), BF16),
        grid=(m // tm,),
        in_specs=[pl.BlockSpec((tm, d), lambda i: (i, 0)),
                  pl.BlockSpec((1, d), lambda i: (0, 0))],
        out_specs=pl.BlockSpec((tm, d), lambda i: (i, 0)),
        compiler_params=_cparams(("parallel",)),
        name="prenorm",
    )(x2, g.reshape(1, d))


WCAST_PIECE = 64

assert 2 * WCAST_PIECE == LANES


def _wcast_kernel(a_ref, b_ref, o_ref):
    rows = jnp.concatenate([a_ref[...], b_ref[...]], axis=0)
    o_ref[...] = rows.T.astype(o_ref.dtype)


def _wcast(wt3, piece0, n_cols):
    k = wt3.shape[2]
    piece = lambda off: pl.BlockSpec((None, WCAST_PIECE, k), lambda i: (piece0 + 2 * i + off, 0, 0))
    return pl.pallas_call(
        _wcast_kernel,
        out_shape=jax.ShapeDtypeStruct((k, n_cols), BF16),
        grid=(n_cols // LANES,),
        in_specs=[piece(0), piece(1)],
        out_specs=pl.BlockSpec((k, LANES), lambda i: (0, i)),
        compiler_params=_cparams(("parallel",)),
        name="weight_cast",
    )(wt3, wt3)


def _mm_shift_kernel(h_ref, w_ref, mu_ref, o_ref, carry_ref, *, tiles_per_seq):
    i = pl.program_id(1)
    p = jnp.dot(h_ref[...], w_ref[...], preferred_element_type=F32)

    @pl.when(i % tiles_per_seq == 0)
    def _():
        carry_ref[...] = jnp.zeros_like(carry_ref)

    rows = lax.broadcasted_iota(jnp.int32, p.shape, 0)
    prev = jnp.where(rows == 0, carry_ref[...], pltpu.roll(p, 1, axis=0))
    carry_ref[...] = p[-1:, :]
    o_ref[...] = p + (prev - p) * mu_ref[...]


def _mm_shift(h, w, mu, seq_len, tm=512, tn=1280):
    m, k = h.shape
    n = w.shape[1]
    tm = min(tm, seq_len)
    kern = functools.partial(_mm_shift_kernel, tiles_per_seq=seq_len // tm)
    return pl.pallas_call(
        kern,
        out_shape=jax.ShapeDtypeStruct((m, n), F32),
        grid=(n // tn, m // tm),
        in_specs=[pl.BlockSpec((tm, k), lambda j, i: (i, 0)),
                  pl.BlockSpec((k, tn), lambda j, i: (0, j)),
                  pl.BlockSpec((1, tn), lambda j, i: (0, j))],
        out_specs=pl.BlockSpec((tm, tn), lambda j, i: (i, j)),
        scratch_shapes=[pltpu.VMEM((1, tn), F32)],
        compiler_params=_cparams(("parallel", "arbitrary")),
        name="inproj_shift",
    )(h, w, mu)


def _mm_gates_kernel(h_ref, wgr_ref, wgv_ref, wgg_ref, wgc_ref, gr_ref, gc_ref, u_ref):
    h = h_ref[...]
    dot = lambda w_ref: jnp.dot(h, w_ref[...], preferred_element_type=F32)
    pr = dot(wgr_ref)
    gr_ref[...] = pr * jax.nn.sigmoid(pr)
    pc = dot(wgc_ref)
    gc_ref[...] = pc * jax.nn.sigmoid(pc)
    u_ref[...] = dot(wgv_ref) * jax.nn.sigmoid(dot(wgg_ref))


def _mm_gates(h, w, width, tm=1024, tn=256):
    m, k = h.shape
    tm = min(tm, m)
    per = width // tn
    wspec = lambda g: pl.BlockSpec((k, tn), lambda i, j: (0, g * per + j))
    ospec = pl.BlockSpec((tm, tn), lambda i, j: (i, j))
    oshape = jax.ShapeDtypeStruct((m, width), F32)
    return pl.pallas_call(
        _mm_gates_kernel,
        out_shape=(oshape, oshape, oshape),
        grid=(m // tm, per),
        in_specs=[pl.BlockSpec((tm, k), lambda i, j: (i, 0)), wspec(0), wspec(1), wspec(2), wspec(3)],
        out_specs=(ospec, ospec, ospec),
        compiler_params=_cparams(("parallel", "arbitrary")),
        name="inproj_gates",
    )(h, w, w, w, w)


def _bf(z):
    return z.astype(BF16)


def _mm(a, b):
    return jnp.dot(a, b, preferred_element_type=F32)


def _mm_nt(a, b):
    return lax.dot_general(a, b, (((1,), (1,)), ((), ())), preferred_element_type=F32)


def _cumsum_rows(cum_l, z):
    hi = _bf(z)
    r1 = z - hi.astype(F32)
    mid = _bf(r1)
    lo = _bf(r1 - mid.astype(F32))
    two = _mm(cum_l, jnp.concatenate([hi, mid], axis=1))
    return two[:, :LANES] + two[:, LANES:] + _mm(cum_l, lo)


def _stack(z, head0):
    zb = _bf(z)
    zero = jnp.zeros_like(zb)
    return jnp.concatenate([jnp.where(head0, zb, zero), jnp.where(head0, zero, zb)], axis=0)


def _unit_lower_inverse(a_list, rows, cols, head0):
    eye = (rows == cols).astype(F32)
    same8 = (rows // INV_BASE) == (cols // INV_BASE)
    a8 = [jnp.where(same8, a, 0.0) for a in a_list]
    a8_s = [_stack(z, head0) for z in a8]
    a8_2 = [_mm(_bf(z), zs) for z, zs in zip(a8, a8_s)]
    yield
    a8_2s = [_stack(z, head0) for z in a8_2]
    a8_4s = [_stack(_mm(_bf(z), zs), head0) for z, zs in zip(a8_2, a8_2s)]
    yield
    p1 = [eye + z for z in a8]
    p1 = [p + _mm(_bf(p), zs) for p, zs in zip(p1, a8_2s)]
    yield
    t = [p + _mm(_bf(p), zs) for p, zs in zip(p1, a8_4s)]
    yield
    blk = INV_BASE
    while blk < CHUNK:
        inner = (rows // blk) == (cols // blk)
        outer = (rows // (2 * blk)) == (cols // (2 * blk))
        join = outer & jnp.logical_not(inner)
        e_s = [_stack(jnp.where(join, a, 0.0), head0) for a in a_list]
        t_b = [_bf(z) for z in t]
        t_s = [_stack(z, head0) for z in t]
        te = [_bf(_mm(tb, es)) for tb, es in zip(t_b, e_s)]
        yield
        t = [z + _mm(tez, ts) for z, tez, ts in zip(t, te, t_s)]
        yield
        blk *= 2
    return t


def _interleave(*gens):
    live = list(gens)
    while live:
        for gen in list(live):
            try:
                next(gen)
            except StopIteration:
                live.remove(gen)


def _rwkv_kernel(r_ref, k_ref, v_ref, lora_ref, gate_ref, w0_ref, a0_ref, kk_ref, ka_ref,
                 rk_ref, lng_ref, lnb_ref, wup_ref, aup_ref, o_ref, s_ref, y_ref, *, n_chunks,
                 n_pairs):
    @pl.when(pl.program_id(2) == 0)
    def _():
        s_ref[...] = jnp.zeros_like(s_ref)

    lane = lax.broadcasted_iota(jnp.int32, (1, LANES), 1)
    head0 = lane < HEAD_SIZE
    rows = lax.broadcasted_iota(jnp.int32, (CHUNK, LANES), 0)
    cols = lax.broadcasted_iota(jnp.int32, (CHUNK, LANES), 1) % HEAD_SIZE
    strict = cols < rows
    incl = cols <= rows
    incl2 = jnp.concatenate([incl, incl], axis=1)
    cr = lax.broadcasted_iota(jnp.int32, (CHUNK, CHUNK), 0)
    cc = lax.broadcasted_iota(jnp.int32, (CHUNK, CHUNK), 1)
    cum_l = _bf((cc <= cr).astype(F32))

    def seg_sum(z):
        s0 = jnp.sum(jnp.where(head0, z, 0.0), axis=-1, keepdims=True)
        s1 = jnp.sum(jnp.where(head0, 0.0, z), axis=-1, keepdims=True)
        return jnp.where(head0, s0, s1)

    def stack(z):
        return _stack(z, head0)

    def pair_transpose(z):
        zero = jnp.zeros_like(z)
        bd = jnp.concatenate([jnp.where(head0, z, zero), jnp.where(head0, zero, z)], axis=0)
        bd_t = bd.T
        return bd_t[:CHUNK] + bd_t[CHUNK:]

    lo = lora_ref[...]
    zw = w0_ref[...] + _mm(_bf(jnp.tanh(lo)), wup_ref[...])
    nz = -zw
    softplus = jnp.maximum(nz, 0.0) + jnp.log(1.0 + jnp.exp(-jnp.abs(nz)))
    w_log = -softplus - 0.5
    lw_all = -jnp.exp(w_log)
    alr_all = jax.nn.sigmoid(a0_ref[...] + _mm(_bf(lo), aup_ref[...]))

    lanes_of = [slice(p * LANES, (p + 1) * LANES) for p in range(n_pairs)]
    rsl = lambda c: slice(c * CHUNK, (c + 1) * CHUNK)
    waves = [list(range(0, n_pairs // 2)), list(range(n_pairs // 2, n_pairs))]
    units_of = [[(p, c) for c in range(n_chunks) for p in wave] for wave in waves]
    g_of = [[_cumsum_rows(cum_l, lw_all[rsl(c), lanes_of[p]]) for p, c in units]
            for units in units_of]
    pre = {}
    bonus = {}

    def prepare(wave):
        r, v, lw, kh, bb, aa = {}, {}, {}, {}, {}, {}
        for p in waves[wave]:
            ls = lanes_of[p]
            r_p = r_ref[:, ls]
            k_p = k_ref[:, ls]
            v_p = v_ref[:, ls]
            alr = alr_all[:, ls]
            kkr = k_p * kk_ref[:, ls]
            kk = kkr * lax.rsqrt(jnp.maximum(seg_sum(kkr * kkr), 1e-24))
            kh_p = k_p * (1.0 + (alr - 1.0) * ka_ref[:, ls])
            r[p], v[p], lw[p], kh[p], bb[p], aa[p] = r_p, v_p, lw_all[:, ls], kh_p, kk * alr, -kk
            bonus[p] = seg_sum(r_p * kh_p * rk_ref[:, ls]) * v_p
            yield
        for (p, c), g in zip(units_of[wave], g_of[wave]):
            lw_u = lw[p][rsl(c)]
            g_end = g[CHUNK - 1:CHUNK, :]
            e_pos = jnp.exp(g)
            e_neg = jnp.exp(-g)
            e_tail = jnp.exp(g_end - g)
            bb_u, kh_u, v_u = bb[p][rsl(c)], kh[p][rsl(c)], v[p][rsl(c)]
            pre[(p, c)] = dict(
                r_t=_bf(r[p][rsl(c)] * e_pos),
                a_t=_bf(aa[p][rsl(c)] * jnp.exp(g - lw_u)),
                b_ts=stack(bb_u * e_neg), k_ts=stack(kh_u * e_neg),
                b_hs=stack(bb_u * e_tail), k_hs=stack(kh_u * e_tail),
                v_s=stack(v_u), v_t=_bf(pair_transpose(v_u)), decay=jnp.exp(g_end))
            yield

    def precompute(wave):
        units = units_of[wave]
        get = lambda name: [pre[unit][name] for unit in units]
        r_t, a_t, b_ts, k_ts, b_hs, k_hs, v_s, v_t = (
            get(name) for name in ("r_t", "a_t", "b_ts", "k_ts", "b_hs", "k_hs", "v_s", "v_t"))
        gram = [_mm_nt(jnp.concatenate([a, rr], axis=0), jnp.concatenate([b, kx], axis=0))
                for a, rr, b, kx in zip(a_t, r_t, b_ts, k_ts)]
        yield
        a_ab = [jnp.where(strict, z[:CHUNK, :LANES], 0.0) for z in gram]
        a_ak = [_bf(jnp.where(strict, z[:CHUNK, LANES:], 0.0)) for z in gram]
        a_r = [_bf(jnp.where(incl2, z[CHUNK:], 0.0)) for z in gram]
        akv = [_mm(a, vs) for a, vs in zip(a_ak, v_s)]
        yield
        t_inv = yield from _unit_lower_inverse(a_ab, rows, cols, head0)
        w12 = [_mm(_bf(t), jnp.concatenate([stack(a), stack(z)], axis=1))
               for t, a, z in zip(t_inv, a_t, akv)]
        yield
        w1 = [z[:, :LANES] for z in w12]
        w2 = [z[:, LANES:] for z in w12]
        w1_b = [_bf(z) for z in w1]
        w1_t = [_bf(pair_transpose(z)) for z in w1]
        w2_t = [_bf(pair_transpose(z)) for z in w2]
        yield
        m_cs = [stack(_mm(wt, b)) for wt, b in zip(w1_t, b_hs)]
        yield
        n_c = [_mm(jnp.concatenate([wt, vx], axis=1), jnp.concatenate([b, kx], axis=0))
               for wt, vx, b, kx in zip(w2_t, v_t, b_hs, k_hs)]
        for unit, ar, wb, w2u, mc, nc in zip(units, a_r, w1_b, w2, m_cs, n_c):
            pre[unit].update(a_r=ar, w1_b=wb, w2=w2u, m_cs=mc, n_c=nc)
        yield

    def chain(wave):
        s = {p: s_ref[p] for p in waves[wave]}
        us_prev = {}
        for c in range(n_chunks + 1):
            us_now = {}
            if c < n_chunks:
                for p in waves[wave]:
                    q = pre[(p, c)]
                    s_b = _bf(s[p])
                    s_s = stack(s[p])
                    s[p] = s[p] * q["decay"] + _mm(s_b, q["m_cs"]) + q["n_c"]
                    us_now[p] = _mm_nt(jnp.concatenate([q["w1_b"], q["r_t"]], axis=0), s_s)
            if c > 0:
                for p in waves[wave]:
                    q = pre[(p, c - 1)]
                    u = us_prev[p][:CHUNK] + q["w2"]
                    y_ref[rsl(c - 1), lanes_of[p]] = us_prev[p][CHUNK:] + _mm(
                        q["a_r"], jnp.concatenate([stack(u), q["v_s"]], axis=0))
            us_prev = us_now
            yield
        for p in waves[wave]:
            s_ref[p] = s[p]

    _interleave(prepare(0))
    _interleave(precompute(0), prepare(1))
    _interleave(precompute(1), chain(0))
    _interleave(chain(1))

    inv_n = 1.0 / HEAD_SIZE
    for p, ls in enumerate(lanes_of):
        y = y_ref[:, ls]
        mu = seg_sum(y) * inv_n
        d = y - mu
        var = seg_sum(d * d) * inv_n
        yn = d * lax.rsqrt(var + GN_EPS) * lng_ref[:, ls] + lnb_ref[:, ls]
        o_ref[:, ls] = ((yn + bonus[p]) * gate_ref[:, ls]).astype(o_ref.dtype)


def _rwkv(p3, gate3, w0, a0, k_k, k_a, r_k, lnx_g, lnx_b, wup, aup, rwkv_w, tt=128):
    b, t, _ = p3.shape
    width = PAIRS_PER_STEP * LANES
    n_steps = rwkv_w // width
    lora_blk = (3 * rwkv_w) // LORA_PAD
    vec = lambda z: z.reshape(1, rwkv_w).astype(F32)
    col = lambda off: pl.BlockSpec((None, tt, width), lambda bi, hp, ti: (bi, ti, off + hp))
    par = pl.BlockSpec((1, width), lambda bi, hp, ti: (0, hp))
    up = pl.BlockSpec((LORA_PAD, width), lambda bi, hp, ti: (0, hp))
    kern = functools.partial(_rwkv_kernel, n_chunks=tt // CHUNK, n_pairs=PAIRS_PER_STEP)
    return pl.pallas_call(
        kern,
        out_shape=jax.ShapeDtypeStruct((b, t, rwkv_w), BF16),
        grid=(b, n_steps, t // tt),
        in_specs=[col(0), col(n_steps), col(2 * n_steps),
                  pl.BlockSpec((None, tt, LORA_PAD), lambda bi, hp, ti: (bi, ti, lora_blk)),
                  col(0),
                  par, par, par, par, par, par, par, up, up],
        out_specs=col(0),
        scratch_shapes=[pltpu.VMEM((PAIRS_PER_STEP, CHUNK, LANES), F32),
                        pltpu.VMEM((tt, width), F32)],
        compiler_params=_cparams(("parallel", "parallel", "arbitrary")),
        name="rwkv7",
    )(p3, p3, p3, p3, gate3, vec(w0), vec(a0), vec(k_k), vec(k_a), vec(r_k), vec(lnx_g),
      vec(lnx_b), wup, aup)


HALO = 32
CONV_ROWS = 64
CONV_LANES = 128


def _conv_kernel(u_ref, gate_ref, cw_ref, cb_ref, lg_ref, lb_ref, w_ref, b_ref, o_ref, buf_ref,
                 acc_ref, *, tt):
    cw_all = buf_ref.shape[1]

    @pl.when(pl.program_id(1) == 0)
    def _():
        buf_ref[0:HALO, :] = jnp.zeros((HALO, cw_all), F32)

    buf_ref[HALO:HALO + tt, :] = u_ref[...]
    base = HALO - (CONV_K - 1)

    win_rows = CONV_ROWS + HALO

    def lane_block(lb, carry):
        ls = pl.ds(pl.multiple_of(lb * CONV_LANES, CONV_LANES), CONV_LANES)
        bias = cb_ref[:, ls]
        for rb in range(tt // CONV_ROWS):
            win = buf_ref[rb * CONV_ROWS:rb * CONV_ROWS + win_rows, ls]
            acc = jnp.zeros((CONV_ROWS, CONV_LANES), F32) + bias
            for res in range(SUBLANES):
                sub = (base + res) % SUBLANES
                rot = win if sub == 0 else pltpu.roll(win, win_rows - sub, axis=0)
                for j in range(res, CONV_K, SUBLANES):
                    lo_row = base + j - sub
                    acc = acc + rot[lo_row:lo_row + CONV_ROWS] * cw_ref[pl.ds(j, 1), ls]
            acc_ref[rb * CONV_ROWS:(rb + 1) * CONV_ROWS, ls] = acc
        return carry

    lax.fori_loop(0, cw_all // CONV_LANES, lane_block, 0)
    tail = buf_ref[tt:tt + HALO, :]
    buf_ref[0:HALO, :] = tail

    acc = acc_ref[...]
    mu = jnp.mean(acc, axis=-1, keepdims=True)
    d = acc - mu
    var = jnp.mean(d * d, axis=-1, keepdims=True)
    c = d * lax.rsqrt(var + LN_EPS) * lg_ref[...] + lb_ref[...]
    c = c * jax.nn.sigmoid(c)
    z = jnp.dot(c.astype(BF16), w_ref[...], preferred_element_type=F32) + b_ref[...]
    o_ref[...] = (z * gate_ref[...]).astype(o_ref.dtype)


def _conv(u3, gate3, conv_w, conv_b, cln_g, cln_b, w_pw2, b_pw2, tt=256):
    b, t, cw = u3.shape
    vec = lambda z: z.reshape(1, cw).astype(F32)
    par = pl.BlockSpec((1, cw), lambda bi, ti: (0, 0))
    row = pl.BlockSpec((None, tt, cw), lambda bi, ti: (bi, ti, 0))
    kern = functools.partial(_conv_kernel, tt=tt)
    return pl.pallas_call(
        kern,
        out_shape=jax.ShapeDtypeStruct((b, t, cw), BF16),
        grid=(b, t // tt),
        in_specs=[row, row,
                  pl.BlockSpec((CONV_K, cw), lambda bi, ti: (0, 0)),
                  par, par, par,
                  pl.BlockSpec((cw, cw), lambda bi, ti: (0, 0)),
                  par],
        out_specs=row,
        scratch_shapes=[pltpu.VMEM((HALO + tt, cw), F32), pltpu.VMEM((tt, cw), F32)],
        compiler_params=_cparams(("parallel", "arbitrary")),
        name="conv_group",
    )(u3, gate3, conv_w.astype(F32), vec(conv_b), vec(cln_g), vec(cln_b), w_pw2.astype(BF16),
      vec(b_pw2))


def _outproj_kernel(ya_ref, yb_ref, wa_ref, wb_ref, x_ref, g_ref, o_ref, *, tn, n_steps):
    j = pl.program_id(1)
    acc = jnp.dot(ya_ref[...], wa_ref[...], preferred_element_type=F32)
    acc = acc + jnp.dot(yb_ref[...], wb_ref[...], preferred_element_type=F32)
    o_ref[:, pl.ds(pl.multiple_of(j * tn, tn), tn)] = acc

    @pl.when(j == n_steps - 1)
    def _():
        out = o_ref[...]
        ms = jnp.mean(out * out, axis=-1, keepdims=True)
        o_ref[...] = x_ref[...] + out * lax.rsqrt(ms + NORM_EPS) * g_ref[...]


def _outproj(ya, yb, w, x2, g, tm=512, tn=512):
    m, ka = ya.shape
    kb = yb.shape[1]
    d = w.shape[1]
    kern = functools.partial(_outproj_kernel, tn=tn, n_steps=d // tn)
    return pl.pallas_call(
        kern,
        out_shape=jax.ShapeDtypeStruct((m, d), F32),
        grid=(m // tm, d // tn),
        in_specs=[pl.BlockSpec((tm, ka), lambda i, j: (i, 0)),
                  pl.BlockSpec((tm, kb), lambda i, j: (i, 0)),
                  pl.BlockSpec((ka, tn), lambda i, j: (0, j)),
                  pl.BlockSpec((kb, tn), lambda i, j: (ka // kb, j)),
                  pl.BlockSpec((tm, d), lambda i, j: (i, 0)),
                  pl.BlockSpec((1, d), lambda i, j: (0, 0))],
        out_specs=pl.BlockSpec((tm, d), lambda i, j: (i, 0)),
        compiler_params=_cparams(("parallel", "arbitrary")),
        name="outproj_norm",
    )(ya, yb, w, w, x2, g.reshape(1, d).astype(F32))


def kernel(x, norm_pre_g, w_in, mu_shift, w0, w_lora_up, a0, a_lora_up, k_k, k_a, r_k, lnx_g,
           lnx_b, conv_w, conv_b, cln_g, cln_b, w_pw2, b_pw2, w_out, norm_post_g):
    bsz, seq, d_model = x.shape
    rwkv_w = w0.shape[0]
    conv_wd = conv_b.shape[0]
    assert rwkv_w == conv_wd and rwkv_w % (PAIRS_PER_STEP * LANES) == 0
    assert seq % 512 == 0 and (bsz * seq) % 1024 == 0 and d_model % 512 == 0
    assert w_lora_up.shape[0] == LORA_W and a_lora_up.shape[0] == LORA_A
    assert w_in.shape == (d_model, 3 * rwkv_w + LORA_W + LORA_A + rwkv_w + 3 * conv_wd)
    shift_cols = 3 * rwkv_w + LORA_W + LORA_A
    shift_pad = 3 * rwkv_w + LORA_PAD
    m = bsz * seq
    x2 = x.reshape(m, d_model)

    pad = shift_pad - shift_cols
    assert shift_cols % WCAST_PIECE == 0 and w_in.shape[1] % WCAST_PIECE == 0
    wt3 = w_in.T.reshape(w_in.shape[1] // WCAST_PIECE, WCAST_PIECE, d_model)
    w_shift = _wcast(wt3, 0, shift_pad)
    w_rest = _wcast(wt3, shift_cols // WCAST_PIECE, w_in.shape[1] - shift_cols)
    mu_p = jnp.concatenate([mu_shift, jnp.zeros((pad,), mu_shift.dtype)]).reshape(1, -1).astype(F32)
    wup = jnp.zeros((LORA_PAD, rwkv_w), BF16).at[:LORA_W].set(w_lora_up.astype(BF16))
    aup = jnp.zeros((LORA_PAD, rwkv_w), BF16).at[LORA_W:LORA_W + LORA_A].set(a_lora_up.astype(BF16))

    h = _prenorm(x2, norm_pre_g.astype(F32))
    p = _mm_shift(h, w_shift, mu_p, seq)
    gate_r, gate_c, u = _mm_gates(h, w_rest, rwkv_w)

    to3 = lambda z: z.reshape(bsz, seq, -1)
    y_rwkv = _rwkv(to3(p), to3(gate_r), w0, a0, k_k, k_a, r_k, lnx_g, lnx_b, wup, aup, rwkv_w)
    y_conv = _conv(to3(u), to3(gate_c), conv_w, conv_b, cln_g, cln_b, w_pw2, b_pw2)

    out = _outproj(y_rwkv.reshape(m, rwkv_w), y_conv.reshape(m, conv_wd), w_out.astype(BF16), x2,
                   norm_post_g)
    return out.reshape(bsz, seq, d_model)
```
